```python
import jax, jax.numpy as jnp
from jax import lax
import numpy as np

D_MODEL = 1024
BATCH = 8
SEQ = 2048
DEPTH = 4
DEC_BATCH = 128
DEC_SEQ = 4
PAST_LEN = 2048
PAGE_SIZE = 128

N_A_LAYERS = DEPTH // 2
N_B_LAYERS = DEPTH - N_A_LAYERS
GLA_HEADS = 4
GLA_DK = D_MODEL // 2
GLA_DV = D_MODEL
GLA_DK_HEAD = GLA_DK // GLA_HEADS
GLA_DV_HEAD = GLA_DV // GLA_HEADS
GLA_GATE_RANK = 16
GLA_TAU = 16.0
GLA_CHUNK = 64
SB_HEADS = 16
SB_HEAD_DIM = D_MODEL // SB_HEADS
SB_WIDTH = SB_HEADS * SB_HEAD_DIM
SB_BIAS_INIT = -7.0
Q_BLOCK = 128
RMS_EPS = 1e-6

kernel_name = "yoco_gla_stickbreaking_step"


def rmsnorm(x, g):
    xf = x.astype(jnp.float32)
    y = xf * lax.rsqrt(jnp.mean(xf * xf, axis=-1, keepdims=True) + RMS_EPS)
    return (y * g.astype(jnp.float32)).astype(x.dtype)


def gla_mixer(hn, s0, w_in, w_a2, b_a, g_head, w_out):
    Bn, T, _ = hn.shape
    f32 = jnp.float32
    proj = hn @ w_in
    q, k, v, g, lr = jnp.split(
        proj, [GLA_DK, 2 * GLA_DK, 2 * GLA_DK + GLA_DV, 2 * GLA_DK + 2 * GLA_DV], axis=-1)
    log_a = jax.nn.log_sigmoid((lr @ w_a2 + b_a).astype(f32)) / GLA_TAU
    L = min(GLA_CHUNK, T)
    nc = -(-T // L)
    pad = nc * L - T

    def chunks(a, dh):
        a = jnp.pad(a.astype(f32), ((0, 0), (0, pad), (0, 0)))
        return a.reshape(Bn, nc, L, GLA_HEADS, dh)

    qc = chunks(q, GLA_DK_HEAD) * (GLA_DK_HEAD ** -0.5)
    kc = chunks(k, GLA_DK_HEAD)
    vc = chunks(v, GLA_DV_HEAD)
    b = jnp.cumsum(chunks(log_a, GLA_DK_HEAD), axis=2)
    b_last = b[:, :, -1:]
    q_t = qc * jnp.exp(b)
    k_t = kc * jnp.exp(-b)
    k_e = kc * jnp.exp(b_last - b)
    causal = jnp.tril(jnp.ones((L, L), dtype=bool))
    p = jnp.einsum('bnihd,bnjhd->bnhij', q_t, k_t)
    p = jnp.where(causal, p, 0.0)
    o_intra = jnp.einsum('bnhij,bnjhe->bnihe', p, vc)
    ds = jnp.einsum('bnjhd,bnjhe->nbhde', k_e, vc)
    decay = jnp.exp(b_last[:, :, 0]).transpose(1, 0, 2, 3)

    def step(s, inp):
        d, dsn = inp
        return d[..., None] * s + dsn, s

    s_fin, s_entry = lax.scan(step, s0.astype(f32), (decay, ds))
    o_inter = jnp.einsum('bnihd,nbhde->bnihe', q_t, s_entry)
    o = (o_intra + o_inter).reshape(Bn, nc * L, GLA_HEADS, GLA_DV_HEAD)[:, :T]
    o = rmsnorm(o, g_head).reshape(Bn, T, GLA_DV) * jax.nn.silu(g.astype(f32))
    return o.astype(hn.dtype) @ w_out, s_fin.astype(s0.dtype)


def sb_block(q_blk, q_start, k, v, bias):
    qb = q_blk.shape[1]
    s_len = k.shape[1]
    q_pos = q_start + jnp.arange(qb)
    k_pos = jnp.arange(s_len)
    valid = k_pos[None, :] < q_pos[:, None]
    z = jnp.einsum('bqhd,bkhd->bhqk', q_blk, k) + bias[None, :, None, None]
    lneg = jnp.where(valid, jax.nn.log_sigmoid(-z), 0.0)
    suffix = lax.cumsum(lneg, axis=3, reverse=True) - lneg
    w = jnp.where(valid, jnp.exp(jax.nn.log_sigmoid(z) + suffix), 0.0)
    return jnp.einsum('bhqk,bkhd->bqhd', w, v)


def sb_attention(q, k, v, bias, q_offset):
    Bn, T, H, dh = q.shape
    qb = min(Q_BLOCK, T)
    nb = T // qb
    q_blocks = q.reshape(Bn, nb, qb, H, dh).transpose(1, 0, 2, 3, 4)
    starts = q_offset + qb * jnp.arange(nb, dtype=jnp.int32)
    bias = bias.astype(jnp.float32)
    out = lax.map(lambda a: sb_block(a[0], a[1], k, v, bias), (q_blocks, starts))
    return out.transpose(1, 0, 2, 3, 4).reshape(Bn, T, H, dh)


def run_trunk(x, s_init, past_k, past_v, pos0, g_norm_a, w_in_a, w_alpha2, b_alpha, g_head_a,
              w_out_a, g_kv, w_kv, g_norm_b, w_in_b, b_sb, w_out_b, g_final):
    Bn, T, _ = x.shape
    f32 = jnp.float32
    h = x
    new_s = []
    k_sh = v_sh = keys = vals = None
    for layer in range(DEPTH):
        if layer < N_A_LAYERS:
            i = layer
            out, s_new = gla_mixer(rmsnorm(h, g_norm_a[i]), s_init[i], w_in_a[i], w_alpha2[i],
                                   b_alpha[i], g_head_a[i], w_out_a[i])
            h = h + out
            new_s.append(s_new)
        else:
            if layer == N_A_LAYERS:
                kv = rmsnorm(h, g_kv) @ w_kv
                k_sh = kv[..., :SB_WIDTH].reshape(Bn, T, SB_HEADS, SB_HEAD_DIM)
                v_sh = kv[..., SB_WIDTH:].reshape(Bn, T, SB_HEADS, SB_HEAD_DIM)
                if past_k is None:
                    keys, vals = k_sh.astype(f32), v_sh.astype(f32)
                else:
                    keys = jnp.concatenate([past_k.astype(f32), k_sh.astype(f32)], axis=1)
                    vals = jnp.concatenate([past_v.astype(f32), v_sh.astype(f32)], axis=1)
            i = layer - N_A_LAYERS
            proj = rmsnorm(h, g_norm_b[i]) @ w_in_b[i]
            q = proj[..., :SB_WIDTH].reshape(Bn, T, SB_HEADS, SB_HEAD_DIM).astype(f32) * (SB_HEAD_DIM ** -0.5)
            gate = proj[..., SB_WIDTH:].astype(f32)
            o = sb_attention(q, keys, vals, b_sb[i], pos0).reshape(Bn, T, SB_WIDTH) * jax.nn.silu(gate)
            h = h + o.astype(h.dtype) @ w_out_b[i]
    return rmsnorm(h, g_final), jnp.stack(new_s), k_sh, v_sh


def setup_inputs(seed: int = 0) -> dict:
    key = jax.random.key(seed)
    ks = jax.random.split(key, 20)
    n_pages = PAST_LEN // PAGE_SIZE
    n_used = DEC_BATCH * n_pages
    n_phys = n_used + n_used // 4
    a_in = 2 * GLA_DK + 2 * GLA_DV + GLA_GATE_RANK
    nrm = jax.random.normal
    f32 = jnp.float32
    page_table = jax.random.permutation(ks[5], n_phys)[:n_used].reshape(DEC_BATCH, n_pages).astype(jnp.int32)
    return {
        "x_prompt": nrm(ks[0], (BATCH, SEQ, D_MODEL), f32),
        "x_sample": nrm(ks[1], (DEC_BATCH, DEC_SEQ, D_MODEL), f32),
        "state_gla": nrm(ks[2], (N_A_LAYERS, DEC_BATCH, GLA_HEADS, GLA_DK_HEAD, GLA_DV_HEAD), f32),
        "cache_k": nrm(ks[3], (n_phys, PAGE_SIZE, SB_HEADS, SB_HEAD_DIM), f32),
        "cache_v": nrm(ks[4], (n_phys, PAGE_SIZE, SB_HEADS, SB_HEAD_DIM), f32),
        "page_table": page_table,
        "g_norm_a": 1.0 + 0.05 * nrm(ks[6], (N_A_LAYERS, D_MODEL), f32),
        "w_in_a": nrm(ks[7], (N_A_LAYERS, D_MODEL, a_in), f32) * D_MODEL ** -0.5,
        "w_alpha2": nrm(ks[8], (N_A_LAYERS, GLA_GATE_RANK, GLA_DK), f32) * GLA_GATE_RANK ** -0.5,
        "b_alpha": 0.1 * nrm(ks[9], (N_A_LAYERS, GLA_DK), f32),
        "g_head_a": 1.0 + 0.05 * nrm(ks[10], (N_A_LAYERS, GLA_DV_HEAD), f32),
        "w_out_a": nrm(ks[11], (N_A_LAYERS, GLA_DV, D_MODEL), f32) * GLA_DV ** -0.5,
        "g_kv": 1.0 + 0.05 * nrm(ks[12], (D_MODEL,), f32),
        "w_kv": nrm(ks[13], (D_MODEL, 2 * SB_WIDTH), f32) * D_MODEL ** -0.5,
        "g_norm_b": 1.0 + 0.05 * nrm(ks[14], (N_B_LAYERS, D_MODEL), f32),
        "w_in_b": nrm(ks[15], (N_B_LAYERS, D_MODEL, 2 * SB_WIDTH), f32) * D_MODEL ** -0.5,
        "b_sb": SB_BIAS_INIT + 0.5 * nrm(ks[18], (N_B_LAYERS, SB_HEADS), f32),
        "w_out_b": nrm(ks[16], (N_B_LAYERS, SB_WIDTH, D_MODEL), f32) * SB_WIDTH ** -0.5,
        "g_final": 1.0 + 0.05 * nrm(ks[17], (D_MODEL,), f32),
    }


def reference(x_prompt, x_sample, state_gla, cache_k, cache_v, page_table, g_norm_a, w_in_a, w_alpha2,
              b_alpha, g_head_a, w_out_a, g_kv, w_kv, g_norm_b, w_in_b, b_sb, w_out_b, g_final):
    weights = (g_norm_a, w_in_a, w_alpha2, b_alpha, g_head_a, w_out_a, g_kv, w_kv, g_norm_b, w_in_b, b_sb,
               w_out_b, g_final)
    s_zero = jnp.zeros((N_A_LAYERS, x_prompt.shape[0], GLA_HEADS, GLA_DK_HEAD, GLA_DV_HEAD), x_prompt.dtype)
    y_prompt, state_gla_prompt, k_prompt, v_prompt = run_trunk(x_prompt, s_zero, None, None, 0, *weights)
    n_seq, n_pages = page_table.shape
    past_len = n_pages * cache_k.shape[1]
    past_k = cache_k[page_table].reshape(n_seq, past_len, SB_HEADS, SB_HEAD_DIM)
    past_v = cache_v[page_table].reshape(n_seq, past_len, SB_HEADS, SB_HEAD_DIM)
    y_sample, state_gla_sample, k_sample, v_sample = run_trunk(x_sample, state_gla, past_k, past_v, past_len, *weights)
    return (y_prompt, y_sample, state_gla_prompt, state_gla_sample, k_prompt, v_prompt, k_sample, v_sample)
```

```python
import functools

import jax
import jax.numpy as jnp
from jax import lax
from jax.experimental import pallas as pl
from jax.experimental.pallas import tpu as pltpu

F32 = jnp.float32
BF16 = jnp.bfloat16

RMS_EPS = 1e-6
GLA_HEADS = 4
GLA_TAU = 16.0
GLA_CHUNK = 64
SB_HEADS = 16
LANES = 128
SAMPLE_CHUNK = 16
VMEM_LIMIT = 56 * 1024 * 1024

_NT = (((1,), (1,)), ((), ()))
_TN = (((0,), (0,)), ((), ()))


def _dot(a, b):
    return jnp.dot(a, b, preferred_element_type=F32)


def _dot_nt(a, b):
    return lax.dot_general(a, b, _NT, preferred_element_type=F32)


def _dot_tn(a, b):
    return lax.dot_general(a, b, _TN, preferred_element_type=F32)


def _split_bf16(x):
    hi = x.astype(BF16)
    lo = (x - hi.astype(F32)).astype(BF16)
    return hi, lo


def _xhat(h):
    return h * lax.rsqrt(jnp.mean(h * h, axis=-1, keepdims=True) + RMS_EPS)


def _silu(x):
    return x / (1.0 + jnp.exp(-x))


def _softplus(z):
    return jnp.maximum(z, 0.0) + jnp.log(1.0 + jnp.exp(-jnp.abs(z)))


def _params(*sem):
    return pltpu.CompilerParams(dimension_semantics=sem, vmem_limit_bytes=VMEM_LIMIT)


def _row_spec(tm, n):
    return pl.BlockSpec((tm, n), lambda i: (i, 0))


def _full_spec(shape):
    return pl.BlockSpec(shape, lambda i: (0,) * len(shape))


def _resid(refs, has_resid):
    if has_resid:
        h_ref, a_ref, wo_ref = refs[:3]
        return h_ref[...] + _dot(a_ref[...].astype(BF16), wo_ref[...]), refs[3:]
    return refs[0][...], refs[1:]


def _gla_in_body(has_resid, dk, dv, *refs):
    h, refs = _resid(refs, has_resid)
    g_ref, w_ref, wlr_ref, wa2_ref, ba_ref = refs[:5]
    outs = refs[5:]
    if has_resid:
        outs[0][...] = h
        outs = outs[1:]
    qk_ref, v_ref, gate_ref, la_ref = outs
    xn = (_xhat(h) * g_ref[...]).astype(BF16)
    qk_ref[...] = _dot(xn, w_ref[:, 0:2 * dk])
    v_ref[...] = _dot(xn, w_ref[:, 2 * dk:2 * dk + dv]).astype(BF16)
    gate_ref[...] = _dot(xn, w_ref[:, 2 * dk + dv:2 * dk + 2 * dv])
    lr = _dot(xn, wlr_ref[...])
    zz = _dot(lr.astype(BF16), wa2_ref[...]) + ba_ref[...]
    la_ref[...] = (-_softplus(-zz)) * (1.0 / GLA_TAU)


def _gla_in_stage(h, a, w_o, g, w_main, w_lr, w_a2, b_a, tm):
    m, d = h.shape
    dk = w_a2.shape[1]
    dv = (w_main.shape[1] - 2 * dk) // 2
    has_resid = a is not None
    ins, specs = [h], [_row_spec(tm, d)]
    if has_resid:
        ins += [a, w_o]
        specs += [_row_spec(tm, a.shape[1]), _full_spec(w_o.shape)]
    ins += [g, w_main, w_lr, w_a2, b_a]
    specs += [_full_spec(x.shape) for x in (g, w_main, w_lr, w_a2, b_a)]
    out_shape, out_specs = [], []
    if has_resid:
        out_shape.append(jax.ShapeDtypeStruct((m, d), F32))
        out_specs.append(_row_spec(tm, d))
    out_shape += [jax.ShapeDtypeStruct((m, 2 * dk), F32), jax.ShapeDtypeStruct((m, dv), BF16),
                  jax.ShapeDtypeStruct((m, dv), F32), jax.ShapeDtypeStruct((m, dk), F32)]
    out_specs += [_row_spec(tm, 2 * dk), _row_spec(tm, dv), _row_spec(tm, dv), _row_spec(tm, dk)]
    res = pl.pallas_call(
        functools.partial(_gla_in_body, has_resid, dk, dv),
        grid=(m // tm,), in_specs=specs, out_specs=out_specs, out_shape=out_shape,
        compiler_params=_params("parallel"), name="gla_in_stage")(*ins)
    if not has_resid:
        res = [h] + list(res)
    return res


def _sb_in_body(with_kv, width, q_scale, *refs):
    h, refs = _resid(refs, True)
    if with_kv:
        gkv_ref, wkv_ref, gb_ref, wb_ref = refs[:4]
        h_out, k_ref, v_ref, kb_ref, vb_ref, q_ref, gate_ref = refs[4:]
    else:
        gb_ref, wb_ref = refs[:2]
        h_out, q_ref, gate_ref = refs[2:]
    h_out[...] = h
    xh = _xhat(h)
    if with_kv:
        xkv = (xh * gkv_ref[...]).astype(BF16)
        k = _dot(xkv, wkv_ref[:, 0:width])
        v = _dot(xkv, wkv_ref[:, width:2 * width])
        k_ref[...] = k
        v_ref[...] = v
        kb_ref[...] = k.astype(BF16)
        vb_ref[...] = v.astype(BF16)
    xb = (xh * gb_ref[...]).astype(BF16)
    q_ref[...] = (_dot(xb, wb_ref[:, 0:width]) * q_scale).astype(BF16)
    gate_ref[...] = _dot(xb, wb_ref[:, width:2 * width])


def _sb_in_stage(h, a, w_o, g_kv, w_kv, g_b, w_b, tm):
    m, d = h.shape
    width = w_b.shape[1] // 2
    with_kv = w_kv is not None
    ins = [h, a, w_o]
    specs = [_row_spec(tm, d), _row_spec(tm, a.shape[1]), _full_spec(w_o.shape)]
    if with_kv:
        ins += [g_kv, w_kv]
        specs += [_full_spec(g_kv.shape), _full_spec(w_kv.shape)]
    ins += [g_b, w_b]
    specs += [_full_spec(g_b.shape), _full_spec(w_b.shape)]
    out_shape = [jax.ShapeDtypeStruct((m, d), F32)]
    out_specs = [_row_spec(tm, d)]
    if with_kv:
        out_shape += [jax.ShapeDtypeStruct((m, width), F32)] * 2 + [jax.ShapeDtypeStruct((m, width), BF16)] * 2
        out_specs += [_row_spec(tm, width)] * 4
    out_shape += [jax.ShapeDtypeStruct((m, width), BF16), jax.ShapeDtypeStruct((m, width), F32)]
    out_specs += [_row_spec(tm, width)] * 2
    head_dim = width // SB_HEADS
    return pl.pallas_call(
        functools.partial(_sb_in_body, with_kv, width, head_dim ** -0.5),
        grid=(m // tm,), in_specs=specs, out_specs=out_specs, out_shape=out_shape,
        compiler_params=_params("parallel"), name="sb_in_stage")(*ins)


def _final_body(h_ref, a_ref, wo_ref, g_ref, y_ref):
    h = h_ref[...] + _dot(a_ref[...].astype(BF16), wo_ref[...])
    y_ref[...] = _xhat(h) * g_ref[...]


def _final_stage(h, a, w_o, g, tm):
    m, d = h.shape
    return pl.pallas_call(
        _final_body, grid=(m // tm,),
        in_specs=[_row_spec(tm, d), _row_spec(tm, a.shape[1]), _full_spec(w_o.shape), _full_spec(g.shape)],
        out_specs=_row_spec(tm, d), out_shape=jax.ShapeDtypeStruct((m, d), F32),
        compiler_params=_params("parallel"), name="final_stage")(h, a, w_o, g)


def _gla_body(has_s0, chunk, n_chunks, q_scale, *refs):
    if has_s0:
        s0_ref, refs = refs[0], refs[1:]
    q_ref, k_ref, v_ref, la_ref, gate_ref, gh_ref, og_ref, sout_ref, st_ref = refs
    c = pl.program_id(2)

    @pl.when(c == 0)
    def _():
        if has_s0:
            st_ref[...] = s0_ref[0, 0].T
        else:
            st_ref[...] = jnp.zeros_like(st_ref)

    row = lax.broadcasted_iota(jnp.int32, (chunk, chunk), 0)
    col = lax.broadcasted_iota(jnp.int32, (chunk, chunk), 1)
    causal = row >= col
    tri = jnp.where(causal, 1.0, 0.0).astype(BF16)
    for n in range(n_chunks):
        sl = pl.ds(n * chunk, chunk)
        hi, lo = _split_bf16(la_ref[sl, :])
        b = _dot(tri, hi) + _dot(tri, lo)
        b_last = b[chunk - 1:chunk, :]
        q_t = (q_ref[sl, :] * q_scale * jnp.exp(b)).astype(BF16)
        kk = k_ref[sl, :]
        k_t = (kk * jnp.exp(-b)).astype(BF16)
        k_e = (kk * jnp.exp(b_last - b)).astype(BF16)
        vv = v_ref[sl, :]
        p = jnp.where(causal, _dot_nt(q_t, k_t), 0.0).astype(BF16)
        st = st_ref[...]
        o = _dot(p, vv) + _dot_nt(q_t, st.astype(BF16))
        st_ref[...] = jnp.exp(b_last) * st + _dot_tn(vv, k_e)
        on = _xhat(o) * gh_ref[...]
        og_ref[sl, :] = (on * _silu(gate_ref[sl, :])).astype(BF16)

    @pl.when(c == pl.num_programs(2) - 1)
    def _():
        sout_ref[0, 0] = st_ref[...].T


def _gla(qk, v, la, gate, g_head, s0, n_seq, t_len, chunk, tc):
    dk = la.shape[1]
    dv = v.shape[1]
    dkh, dvh = dk // GLA_HEADS, dv // GLA_HEADS
    nct = t_len // tc
    has_s0 = s0 is not None
    tok = lambda b, h, c: (b * nct + c, h)
    ins, specs = [], []
    if has_s0:
        ins.append(s0)
        specs.append(pl.BlockSpec((1, 1, dkh, dvh), lambda b, h, c: (b, h, 0, 0)))
    ins += [qk, qk, v, la, gate, g_head]
    specs += [pl.BlockSpec((tc, dkh), tok),
              pl.BlockSpec((tc, dkh), lambda b, h, c: (b * nct + c, GLA_HEADS + h)),
              pl.BlockSpec((tc, dvh), tok), pl.BlockSpec((tc, dkh), tok), pl.BlockSpec((tc, dvh), tok),
              pl.BlockSpec((1, dvh), lambda b, h, c: (0, 0))]
    return pl.pallas_call(
        functools.partial(_gla_body, has_s0, chunk, tc // chunk, dkh ** -0.5),
        grid=(n_seq, GLA_HEADS, nct), in_specs=specs,
        out_specs=[pl.BlockSpec((tc, dvh), tok),
                   pl.BlockSpec((1, 1, dkh, dvh), lambda b, h, c: (b, h, 0, 0))],
        out_shape=[jax.ShapeDtypeStruct((n_seq * t_len, dv), BF16),
                   jax.ShapeDtypeStruct((n_seq, GLA_HEADS, dkh, dvh), F32)],
        scratch_shapes=[pltpu.VMEM((dvh, dkh), F32)],
        compiler_params=_params("parallel", "parallel", "arbitrary"), name="gla")(*ins)


def _sb_weights(z, carry, u_incl, valid):
    lneg = -_softplus(z)
    if valid is not None:
        lneg = jnp.where(valid, lneg, 0.0)
    hi, lo = _split_bf16(lneg)
    incl = _dot(hi, u_incl) + _dot(lo, u_incl)
    w = jnp.exp(z + incl + carry)
    if valid is not None:
        w = jnp.where(valid, w, 0.0)
    return w.astype(BF16), jnp.sum(lneg, axis=1, keepdims=True)


def _upper_incl(n):
    j = lax.broadcasted_iota(jnp.int32, (n, n), 0)
    s = lax.broadcasted_iota(jnp.int32, (n, n), 1)
    return jnp.where(j >= s, 1.0, 0.0).astype(BF16)


def _sb_prompt_body(blk, bias_ref, q_ref, k_ref, v_ref, gate_ref, out_ref, o_acc, c_acc):
    pair = pl.program_id(1)
    i = pl.program_id(2)
    head_dim = LANES // 2
    lane = lax.broadcasted_iota(jnp.int32, (1, LANES), 1)
    q = q_ref[...]
    zero = jnp.zeros_like(q)
    q_heads = (jnp.where(lane < head_dim, q, zero), jnp.where(lane >= head_dim, q, zero))
    u_incl = _upper_incl(blk)
    o_acc[...] = jnp.zeros_like(o_acc)
    c_acc[...] = jnp.zeros_like(c_acc)
    r = lax.broadcasted_iota(jnp.int32, (blk, blk), 0)
    s = lax.broadcasted_iota(jnp.int32, (blk, blk), 1)
    diag_valid = s < r

    def block(j, valid):
        ks = k_ref[pl.ds(pl.multiple_of(j * blk, blk), blk), :]
        vs = v_ref[pl.ds(pl.multiple_of(j * blk, blk), blk), :]
        for hh in range(2):
            z = _dot_nt(q_heads[hh], ks) + bias_ref[2 * pair + hh]
            w, lsum = _sb_weights(z, c_acc[hh], u_incl, valid)
            o_acc[hh] += _dot(w, vs)
            c_acc[hh] += lsum

    block(i, diag_valid)

    def body(jj, carry):
        block(i - 1 - jj, None)
        return carry

    lax.fori_loop(0, i, body, 0)
    o = jnp.where(lane < head_dim, o_acc[0], o_acc[1])
    out_ref[...] = (o * _silu(gate_ref[...])).astype(BF16)


def _sb_prompt(q, kb, vb, gate, bias, n_seq, t_len, blk):
    m, width = q.shape
    nq = t_len // blk
    tok = lambda b, p, i, bias_ref: (b * nq + i, p)
    seq = lambda b, p, i, bias_ref: (b, p)
    grid_spec = pltpu.PrefetchScalarGridSpec(
        num_scalar_prefetch=1, grid=(n_seq, width // LANES, nq),
        in_specs=[pl.BlockSpec((blk, LANES), tok), pl.BlockSpec((t_len, LANES), seq),
                  pl.BlockSpec((t_len, LANES), seq), pl.BlockSpec((blk, LANES), tok)],
        out_specs=pl.BlockSpec((blk, LANES), tok),
        scratch_shapes=[pltpu.VMEM((2, blk, LANES), F32), pltpu.VMEM((2, blk, 1), F32)])
    return pl.pallas_call(
        functools.partial(_sb_prompt_body, blk), grid_spec=grid_spec,
        out_shape=jax.ShapeDtypeStruct((m, width), BF16),
        compiler_params=_params("parallel", "parallel", "arbitrary"), name="sb_prompt")(bias, q, kb, vb, gate)


def _sb_sample_body(n_pages, page, t_new, blk, pt_ref, bias_ref, q_ref, knew_ref, vnew_ref, gate_ref, *refs):
    k_pages = refs[:n_pages]
    v_pages = refs[n_pages:2 * n_pages]
    out_ref = refs[2 * n_pages]
    width = q_ref.shape[-1]
    head_dim = width // SB_HEADS
    rows = t_new * SB_HEADS
    head_of_lane = lax.broadcasted_iota(jnp.int32, (SB_HEADS, width), 1) // head_dim
    head_of_row = lax.broadcasted_iota(jnp.int32, (SB_HEADS, width), 0)
    own = head_of_lane == head_of_row
    q = q_ref[0]
    q_bd = jnp.concatenate(
        [jnp.where(own, jnp.broadcast_to(q[t:t + 1, :], (SB_HEADS, width)), 0.0) for t in range(t_new)],
        axis=0).astype(BF16)
    bias = bias_ref[...]
    u_incl = _upper_incl(blk)

    pad = jnp.zeros((page - knew_ref.shape[1], width), F32)
    kn = jnp.concatenate([knew_ref[0], pad], axis=0).astype(BF16)
    vn = jnp.concatenate([vnew_ref[0], pad], axis=0).astype(BF16)
    t_of_row = lax.broadcasted_iota(jnp.int32, (rows, page), 0) // SB_HEADS
    c_idx = lax.broadcasted_iota(jnp.int32, (rows, page), 1)
    z = _dot_nt(q_bd, kn) + bias
    w, carry = _sb_weights(z, 0.0, u_incl[:page, :page], c_idx < t_of_row)
    o = _dot(w, vn)
    per_blk = blk // page
    for j in reversed(range(n_pages // per_blk)):
        kb = jnp.concatenate([k_pages[j * per_blk + u][0] for u in range(per_blk)], axis=0).astype(BF16)
        vb = jnp.concatenate([v_pages[j * per_blk + u][0] for u in range(per_blk)], axis=0).astype(BF16)
        z = _dot_nt(q_bd, kb) + bias
        w, lsum = _sb_weights(z, carry, u_incl, None)
        o = o + _dot(w, vb)
        carry = carry + lsum
    out = jnp.concatenate(
        [jnp.sum(jnp.where(own, o[t * SB_HEADS:(t + 1) * SB_HEADS, :], 0.0), axis=0, keepdims=True)
         for t in range(t_new)], axis=0)
    out_ref[0] = out * _silu(gate_ref[0])


def _sb_sample(q, k_new, v_new, gate, bias_col, cache_k, cache_v, page_table, blk):
    n_seq, t_new, width = q.shape
    n_pages = page_table.shape[1]
    page = cache_k.shape[1]
    pt_flat = page_table.reshape(-1)
    tok = lambda b, pt: (b, 0, 0)

    def page_spec(j):
        return pl.BlockSpec((1, page, width), lambda b, pt: (pt[b * n_pages + j], 0, 0))

    grid_spec = pltpu.PrefetchScalarGridSpec(
        num_scalar_prefetch=1, grid=(n_seq,),
        in_specs=[pl.BlockSpec(bias_col.shape, lambda b, pt: (0, 0)),
                  pl.BlockSpec((1, t_new, width), tok), pl.BlockSpec((1,) + k_new.shape[1:], tok),
                  pl.BlockSpec((1,) + v_new.shape[1:], tok), pl.BlockSpec((1, t_new, width), tok)]
                 + [page_spec(j) for j in range(n_pages)] * 2,
        out_specs=pl.BlockSpec((1, t_new, width), tok))
    return pl.pallas_call(
        functools.partial(_sb_sample_body, n_pages, page, t_new, blk), grid_spec=grid_spec,
        out_shape=jax.ShapeDtypeStruct((n_seq, t_new, width), F32),
        compiler_params=_params("parallel"), name="sb_sample")(
            pt_flat, bias_col, q, k_new, v_new, gate, *([cache_k] * n_pages), *([cache_v] * n_pages))


def _trunk(x, s_init, past, wts, n_seq, t_len, tm):
    (g_norm_a, w_main_a, w_lr_a, w_a2, b_a, g_head_a, w_out_a, g_kv, w_kv, g_norm_b, w_in_b, b_sb,
     w_out_b, g_final) = wts
    n_a = w_main_a.shape[0]
    n_b = w_in_b.shape[0]
    is_prompt = past is None
    h, a, w_o = x, None, None
    states = []
    for i in range(n_a):
        h, qk, v, gate, la = _gla_in_stage(h, a, w_o, g_norm_a[i], w_main_a[i], w_lr_a[i], w_a2[i], b_a[i], tm)
        if is_prompt:
            a, s_new = _gla(qk, v, la, gate, g_head_a[i], None, n_seq, t_len, GLA_CHUNK, 512)
        else:
            def padt(u):
                u = u.reshape(n_seq, t_len, u.shape[-1])
                u = jnp.pad(u, ((0, 0), (0, SAMPLE_CHUNK - t_len), (0, 0)))
                return u.reshape(n_seq * SAMPLE_CHUNK, u.shape[-1])
            a, s_new = _gla(padt(qk), padt(v), padt(la), padt(gate), g_head_a[i], s_init[i], n_seq,
                            SAMPLE_CHUNK, SAMPLE_CHUNK, SAMPLE_CHUNK)
            a = a.reshape(n_seq, SAMPLE_CHUNK, -1)[:, :t_len].reshape(n_seq * t_len, -1)
        states.append(s_new)
        w_o = w_out_a[i]
    k_sh = v_sh = kb = vb = None
    for i in range(n_b):
        if i == 0:
            h, k_sh, v_sh, kb, vb, q, gate = _sb_in_stage(h, a, w_o, g_kv, w_kv, g_norm_b[i], w_in_b[i], tm)
        else:
            h, q, gate = _sb_in_stage(h, a, w_o, None, None, g_norm_b[i], w_in_b[i], tm)
        if is_prompt:
            a = _sb_prompt(q, kb, vb, gate, b_sb[i], n_seq, t_len, 256)
        else:
            cache_k, cache_v, page_table = past
            width = q.shape[1]
            pad8 = lambda u: jnp.pad(u.reshape(n_seq, t_len, width), ((0, 0), (0, 8 - t_len), (0, 0)))
            bias_col = jnp.tile(b_sb[i], t_len).reshape(t_len * SB_HEADS, 1)
            a = _sb_sample(q.astype(F32).reshape(n_seq, t_len, width), pad8(k_sh), pad8(v_sh),
                           gate.reshape(n_seq, t_len, width), bias_col, cache_k, cache_v, page_table, 256)
            a = a.reshape(n_seq * t_len, width)
        w_o = w_out_b[i]
    y = _final_stage(h, a, w_o, g_final, tm)
    return y, jnp.stack(states), k_sh, v_sh


def kernel(x_prompt, x_sample, state_gla, cache_k, cache_v, page_table, g_norm_a, w_in_a, w_alpha2, b_alpha,
           g_head_a, w_out_a, g_kv, w_kv, g_norm_b, w_in_b, b_sb, w_out_b, g_final):
    n_p, t_p, d = x_prompt.shape
    n_s, t_s, _ = x_sample.shape
    dk = w_alpha2.shape[2]
    n_main = w_in_a.shape[2] - w_alpha2.shape[1]
    row = lambda g: g[..., None, :]
    wts = (row(g_norm_a), w_in_a[:, :, :n_main].astype(BF16), w_in_a[:, :, n_main:].astype(BF16),
           w_alpha2.astype(BF16), row(b_alpha), row(g_head_a), w_out_a.astype(BF16), row(g_kv),
           w_kv.astype(BF16), row(g_norm_b), w_in_b.astype(BF16), b_sb, w_out_b.astype(BF16), row(g_final))
    del dk
    sb_shape = (SB_HEADS, w_kv.shape[1] // 2 // SB_HEADS)
    y_p, s_p, k_p, v_p = _trunk(x_prompt.reshape(n_p * t_p, d), None, None, wts, n_p, t_p, 256)
    n_phys, page = cache_k.shape[:2]
    past = (cache_k.reshape(n_phys, page, -1), cache_v.reshape(n_phys, page, -1), page_table)
    y_s, s_s, k_s, v_s = _trunk(x_sample.reshape(n_s * t_s, d), state_gla, past, wts, n_s, t_s, 256)
    return (y_p.reshape(n_p, t_p, d), y_s.reshape(n_s, t_s, d), s_p, s_s,
            k_p.reshape(n_p, t_p, *sb_shape), v_p.reshape(n_p, t_p, *sb_shape),
            k_s.reshape(n_s, t_s, *sb_shape), v_s.reshape(n_s, t_s, *sb_shape))
```

```python
import functools

import jax
import jax.numpy as jnp
from jax import lax
from jax.experimental import pallas as pl
from jax.experimental.pallas import tpu as pltpu

F32 = jnp.float32
BF16 = jnp.bfloat16

RMS_EPS = 1e-6
GLA_HEADS = 4
GLA_TAU = 16.0
GLA_CHUNK = 64
SB_HEADS = 16
LANES = 128
SAMPLE_CHUNK = 16
VMEM_LIMIT = 56 * 1024 * 1024

_NT = (((1,), (1,)), ((), ()))
_TN = (((0,), (0,)), ((), ()))


def _dot(a, b):
    return jnp.dot(a, b, preferred_element_type=F32)


def _dot_nt(a, b):
    return lax.dot_general(a, b, _NT, preferred_element_type=F32)


def _dot_tn(a, b):
    return lax.dot_general(a, b, _TN, preferred_element_type=F32)


def _split_bf16(x):
    hi = x.astype(BF16)
    lo = (x - hi.astype(F32)).astype(BF16)
    return hi, lo


def _xhat(h):
    return h * lax.rsqrt(jnp.mean(h * h, axis=-1, keepdims=True) + RMS_EPS)


def _silu(x):
    return x / (1.0 + jnp.exp(-x))


LOG2_E = 1.4426950408889634


def _softplus(z):
    return jnp.maximum(z, 0.0) + jnp.log(1.0 + jnp.exp2(jnp.abs(z) * (-LOG2_E)))


def _params(*sem):
    return pltpu.CompilerParams(dimension_semantics=sem, vmem_limit_bytes=VMEM_LIMIT)


def _row_spec(tm, n):
    return pl.BlockSpec((tm, n), lambda i: (i, 0))


def _full_spec(shape):
    return pl.BlockSpec(shape, lambda i: (0,) * len(shape))


def _resid(refs, has_resid):
    if has_resid:
        h_ref, a_ref, wo_ref = refs[:3]
        return h_ref[...] + _dot(a_ref[...].astype(BF16), wo_ref[...]), refs[3:]
    return refs[0][...], refs[1:]


def _gla_in_body(has_resid, dk, dv, *refs):
    h, refs = _resid(refs, has_resid)
    g_ref, w_ref, wlr_ref, wa2_ref, ba_ref = refs[:5]
    outs = refs[5:]
    if has_resid:
        outs[0][...] = h
        outs = outs[1:]
    qk_ref, v_ref, gate_ref, la_ref = outs
    xn = (_xhat(h) * g_ref[...]).astype(BF16)
    qk_ref[...] = _dot(xn, w_ref[:, 0:2 * dk])
    v_ref[...] = _dot(xn, w_ref[:, 2 * dk:2 * dk + dv]).astype(BF16)
    gate_ref[...] = _dot(xn, w_ref[:, 2 * dk + dv:2 * dk + 2 * dv])
    lr = _dot(xn, wlr_ref[...])
    zz = _dot(lr.astype(BF16), wa2_ref[...]) + ba_ref[...]
    la_ref[...] = (-_softplus(-zz)) * (1.0 / GLA_TAU)


def _gla_in_stage(h, a, w_o, g, w_main, w_lr, w_a2, b_a, tm):
    m, d = h.shape
    dk = w_a2.shape[1]
    dv = (w_main.shape[1] - 2 * dk) // 2
    has_resid = a is not None
    ins, specs = [h], [_row_spec(tm, d)]
    if has_resid:
        ins += [a, w_o]
        specs += [_row_spec(tm, a.shape[1]), _full_spec(w_o.shape)]
    ins += [g, w_main, w_lr, w_a2, b_a]
    specs += [_full_spec(x.shape) for x in (g, w_main, w_lr, w_a2, b_a)]
    out_shape, out_specs = [], []
    if has_resid:
        out_shape.append(jax.ShapeDtypeStruct((m, d), F32))
        out_specs.append(_row_spec(tm, d))
    out_shape += [jax.ShapeDtypeStruct((m, 2 * dk), F32), jax.ShapeDtypeStruct((m, dv), BF16),
                  jax.ShapeDtypeStruct((m, dv), F32), jax.ShapeDtypeStruct((m, dk), F32)]
    out_specs += [_row_spec(tm, 2 * dk), _row_spec(tm, dv), _row_spec(tm, dv), _row_spec(tm, dk)]
    res = pl.pallas_call(
        functools.partial(_gla_in_body, has_resid, dk, dv),
        grid=(m // tm,), in_specs=specs, out_specs=out_specs, out_shape=out_shape,
        compiler_params=_params("parallel"), name="gla_in_stage")(*ins)
    if not has_resid:
        res = [h] + list(res)
    return res


def _sb_in_body(with_kv, width, q_scale, *refs):
    h, refs = _resid(refs, True)
    if with_kv:
        gkv_ref, wkv_ref, gb_ref, wb_ref = refs[:4]
        h_out, k_ref, v_ref, kb_ref, vb_ref, q_ref, gate_ref = refs[4:]
    else:
        gb_ref, wb_ref = refs[:2]
        h_out, q_ref, gate_ref = refs[2:]
    h_out[...] = h
    xh = _xhat(h)
    if with_kv:
        xkv = (xh * gkv_ref[...]).astype(BF16)
        k = _dot(xkv, wkv_ref[:, 0:width])
        v = _dot(xkv, wkv_ref[:, width:2 * width])
        k_ref[...] = k
        v_ref[...] = v
        kb_ref[...] = k.astype(BF16)
        vb_ref[...] = v.astype(BF16)
    xb = (xh * gb_ref[...]).astype(BF16)
    q_ref[...] = (_dot(xb, wb_ref[:, 0:width]) * q_scale).astype(BF16)
    gate_ref[...] = _dot(xb, wb_ref[:, width:2 * width])


def _sb_in_stage(h, a, w_o, g_kv, w_kv, g_b, w_b, tm):
    m, d = h.shape
    width = w_b.shape[1] // 2
    with_kv = w_kv is not None
    ins = [h, a, w_o]
    specs = [_row_spec(tm, d), _row_spec(tm, a.shape[1]), _full_spec(w_o.shape)]
    if with_kv:
        ins += [g_kv, w_kv]
        specs += [_full_spec(g_kv.shape), _full_spec(w_kv.shape)]
    ins += [g_b, w_b]
    specs += [_full_spec(g_b.shape), _full_spec(w_b.shape)]
    out_shape = [jax.ShapeDtypeStruct((m, d), F32)]
    out_specs = [_row_spec(tm, d)]
    if with_kv:
        out_shape += [jax.ShapeDtypeStruct((m, width), F32)] * 2 + [jax.ShapeDtypeStruct((m, width), BF16)] * 2
        out_specs += [_row_spec(tm, width)] * 4
    out_shape += [jax.ShapeDtypeStruct((m, width), BF16), jax.ShapeDtypeStruct((m, width), F32)]
    out_specs += [_row_spec(tm, width)] * 2
    head_dim = width // SB_HEADS
    return pl.pallas_call(
        functools.partial(_sb_in_body, with_kv, width, head_dim ** -0.5),
        grid=(m // tm,), in_specs=specs, out_specs=out_specs, out_shape=out_shape,
        compiler_params=_params("parallel"), name="sb_in_stage")(*ins)


def _final_body(h_ref, a_ref, wo_ref, g_ref, y_ref):
    h = h_ref[...] + _dot(a_ref[...].astype(BF16), wo_ref[...])
    y_ref[...] = _xhat(h) * g_ref[...]


def _final_stage(h, a, w_o, g, tm):
    m, d = h.shape
    return pl.pallas_call(
        _final_body, grid=(m // tm,),
        in_specs=[_row_spec(tm, d), _row_spec(tm, a.shape[1]), _full_spec(w_o.shape), _full_spec(g.shape)],
        out_specs=_row_spec(tm, d), out_shape=jax.ShapeDtypeStruct((m, d), F32),
        compiler_params=_params("parallel"), name="final_stage")(h, a, w_o, g)


def _gla_body(has_s0, ns, chunk, n_chunks, q_scale, *refs):
    if has_s0:
        s0_ref, refs = refs[0], refs[1:]
    qk_ref, v_ref, la_ref, gate_ref, gh_ref, og_ref, sout_ref, st_ref = refs
    c = pl.program_id(1)
    dk = la_ref.shape[1]
    dkh = dk // GLA_HEADS
    dvh = v_ref.shape[1] // GLA_HEADS

    @pl.when(c == 0)
    def _():
        for s in range(ns):
            for h in range(GLA_HEADS):
                st_ref[s, h] = s0_ref[0, s, h].T if has_s0 else jnp.zeros((dvh, dkh), F32)

    row = lax.broadcasted_iota(jnp.int32, (chunk, chunk), 0)
    col = lax.broadcasted_iota(jnp.int32, (chunk, chunk), 1)
    causal = row >= col
    tri = jnp.where(causal, 1.0, 0.0).astype(BF16)
    for s in range(ns):
        for n in range(n_chunks):
            sl = pl.ds((s * n_chunks + n) * chunk, chunk)
            hi, lo = _split_bf16(la_ref[sl, :])
            b = _dot(tri, hi) + _dot(tri, lo)
            b_last = b[chunk - 1:chunk, :]
            q_t = (qk_ref[sl, :dk] * q_scale * jnp.exp(b)).astype(BF16)
            kk = qk_ref[sl, dk:]
            k_t = (kk * jnp.exp(-b)).astype(BF16)
            k_e = (kk * jnp.exp(b_last - b)).astype(BF16)
            decay = jnp.exp(b_last)
            for h in range(GLA_HEADS):
                kh = slice(h * dkh, (h + 1) * dkh)
                vh = slice(h * dvh, (h + 1) * dvh)
                vv = v_ref[sl, vh]
                p = jnp.where(causal, _dot_nt(q_t[:, kh], k_t[:, kh]), 0.0).astype(BF16)
                st = st_ref[s, h]
                o = _dot(p, vv) + _dot_nt(q_t[:, kh], st.astype(BF16))
                st_ref[s, h] = decay[:, kh] * st + _dot_tn(vv, k_e[:, kh])
                on = _xhat(o) * gh_ref[...]
                og_ref[sl, vh] = (on * _silu(gate_ref[sl, vh])).astype(BF16)

    @pl.when(c == pl.num_programs(1) - 1)
    def _():
        for s in range(ns):
            for h in range(GLA_HEADS):
                sout_ref[s, h] = st_ref[s, h].T


def _gla(qk, v, la, gate, g_head, s0, layer, n_seq, t_len, chunk, tc, ns):
    dk = la.shape[1]
    dv = v.shape[1]
    dkh, dvh = dk // GLA_HEADS, dv // GLA_HEADS
    nct = t_len // tc
    assert ns == 1 or nct == 1
    has_s0 = s0 is not None
    rows = ns * tc
    tok = lambda b, c: (b * nct + c, 0)
    state_spec = pl.BlockSpec((ns, GLA_HEADS, dkh, dvh), lambda b, c: (b, 0, 0, 0))
    ins, specs = [], []
    if has_s0:
        ins.append(s0)
        specs.append(pl.BlockSpec((1, ns, GLA_HEADS, dkh, dvh), lambda b, c: (layer, b, 0, 0, 0)))
    ins += [qk, v, la, gate, g_head]
    specs += [pl.BlockSpec((rows, 2 * dk), tok), pl.BlockSpec((rows, dv), tok), pl.BlockSpec((rows, dk), tok),
              pl.BlockSpec((rows, dv), tok), pl.BlockSpec((1, dvh), lambda b, c: (0, 0))]
    return pl.pallas_call(
        functools.partial(_gla_body, has_s0, ns, chunk, tc // chunk, dkh ** -0.5),
        grid=(n_seq // ns, nct), in_specs=specs,
        out_specs=[pl.BlockSpec((rows, dv), tok), state_spec],
        out_shape=[jax.ShapeDtypeStruct((n_seq * t_len, dv), BF16),
                   jax.ShapeDtypeStruct((n_seq, GLA_HEADS, dkh, dvh), F32)],
        scratch_shapes=[pltpu.VMEM((ns, GLA_HEADS, dvh, dkh), F32)],
        compiler_params=_params("parallel", "arbitrary"), name="gla")(*ins)


def _upper_incl(n):
    j = lax.broadcasted_iota(jnp.int32, (n, n), 0)
    s = lax.broadcasted_iota(jnp.int32, (n, n), 1)
    return jnp.where(j >= s, 1.0, 0.0).astype(BF16)


def _sb_prompt_body(tq, tk, bias_ref, q_ref, k_ref, v_ref, gate_ref, out_ref, u_ref, *scratch):
    qh_ref, o_ref, c_ref = scratch[0:2], scratch[2:4], scratch[4:6]
    z_ref = (scratch[6:8], scratch[8:10])
    w_ref = (scratch[10:12], scratch[12:14])
    pair = pl.program_id(1)
    i = pl.program_id(2)
    head_dim = LANES // 2
    lane = lax.broadcasted_iota(jnp.int32, (1, LANES), 1)
    q = q_ref[...]
    zero = jnp.zeros_like(q)
    qh_ref[0][...] = jnp.where(lane < head_dim, q, zero)
    qh_ref[1][...] = jnp.where(lane >= head_dim, q, zero)
    u_ref[...] = _upper_incl(tk)
    for hh in range(2):
        o_ref[hh][...] = jnp.zeros_like(o_ref[hh])
        c_ref[hh][...] = jnp.zeros_like(c_ref[hh])
    assert tq == 2 * tk

    def kv_rows(j):
        return pl.ds(pl.multiple_of(j * tk, tk), tk)

    def logits(j, slot, r0=0):
        ks = k_ref[kv_rows(j), :]
        for hh in range(2):
            z_ref[slot][hh][r0:, :] = _dot_nt(qh_ref[hh][r0:, :], ks) + bias_ref[2 * pair + hh]

    def weights(j, slot, masked, r0=0):
        if masked:
            r = lax.broadcasted_iota(jnp.int32, (tq - r0, tk), 0) + (i * tq + r0)
            s = lax.broadcasted_iota(jnp.int32, (tq - r0, tk), 1) + j * tk
            valid = s < r
        for hh in range(2):
            sp = _softplus(z_ref[slot][hh][r0:, :])
            if masked:
                sp = jnp.where(valid, sp, 0.0)
            incl = _dot(sp.astype(BF16), u_ref[...])
            w = jnp.exp(z_ref[slot][hh][r0:, :] - incl - c_ref[hh][r0:, :])
            if masked:
                w = jnp.where(valid, w, 0.0)
            w_ref[slot][hh][r0:, :] = w.astype(BF16)
            c_ref[hh][r0:, :] += jnp.sum(sp, axis=1, keepdims=True)

    def pv(j, slot, r0=0):
        vs = v_ref[kv_rows(j), :]
        for hh in range(2):
            o_ref[hh][r0:, :] += _dot(w_ref[slot][hh][r0:, :], vs)

    top = 2 * i + 1
    logits(top, 0, tk)
    weights(top, 0, True, tk)
    logits(top - 1, 1)
    weights(top - 1, 1, True)
    pv(top, 0, tk)
    logits(jnp.maximum(top - 2, 0), 0)

    def body(p, carry):
        j0 = top - 2 * p
        weights(j0, 0, False)
        pv(j0 + 1, 1)
        logits(j0 - 1, 1)
        weights(j0 - 1, 1, False)
        pv(j0, 0)
        logits(jnp.maximum(j0 - 2, 0), 0)
        return carry

    lax.fori_loop(1, i + 1, body, 0)
    pv(0, 1)
    o = jnp.where(lane < head_dim, o_ref[0][...], o_ref[1][...])
    out_ref[...] = (o * _silu(gate_ref[...])).astype(BF16)


def _sb_prompt(q, kb, vb, gate, bias, n_seq, t_len, tq, tk):
    m, width = q.shape
    nq = t_len // tq
    tok = lambda b, p, i, bias_ref: (b * nq + i, p)
    seq = lambda b, p, i, bias_ref: (b, p)
    grid_spec = pltpu.PrefetchScalarGridSpec(
        num_scalar_prefetch=1, grid=(n_seq, width // LANES, nq),
        in_specs=[pl.BlockSpec((tq, LANES), tok), pl.BlockSpec((t_len, LANES), seq),
                  pl.BlockSpec((t_len, LANES), seq), pl.BlockSpec((tq, LANES), tok)],
        out_specs=pl.BlockSpec((tq, LANES), tok),
        scratch_shapes=[pltpu.VMEM((tk, tk), BF16)] + [pltpu.VMEM((tq, LANES), BF16)] * 2
        + [pltpu.VMEM((tq, LANES), F32)] * 2 + [pltpu.VMEM((tq, 1), F32)] * 2
        + [pltpu.VMEM((tq, tk), F32)] * 4 + [pltpu.VMEM((tq, tk), BF16)] * 4)
    return pl.pallas_call(
        functools.partial(_sb_prompt_body, tq, tk), grid_spec=grid_spec,
        out_shape=jax.ShapeDtypeStruct((m, width), BF16),
        compiler_params=_params("parallel", "parallel", "arbitrary"), name="sb_prompt")(bias, q, kb, vb, gate)


def _sb_sample_body(n_pages, page, t_new, pt_ref, bias_ref, q_ref, knew_ref, vnew_ref, gate_ref, *refs):
    kt_pages = refs[:n_pages]
    vt_pages = refs[n_pages:2 * n_pages]
    out_ref = refs[2 * n_pages]
    width = q_ref.shape[-1]
    head_dim = width // SB_HEADS
    rows = t_new * SB_HEADS
    head_of_lane = lax.broadcasted_iota(jnp.int32, (SB_HEADS, width), 1) // head_dim
    head_of_row = lax.broadcasted_iota(jnp.int32, (SB_HEADS, width), 0)
    own = head_of_lane == head_of_row
    q = q_ref[0]
    q_bd = jnp.concatenate(
        [jnp.where(own, jnp.broadcast_to(q[t:t + 1, :], (SB_HEADS, width)), 0.0) for t in range(t_new)],
        axis=0).astype(BF16)
    bias = bias_ref[...]
    blk = 2 * page
    u_incl = _upper_incl(blk)

    pad = jnp.zeros((page - knew_ref.shape[1], width), F32)
    kn = jnp.concatenate([knew_ref[0], pad], axis=0).astype(BF16)
    vn = jnp.concatenate([vnew_ref[0], pad], axis=0).astype(BF16)
    t_of_row = lax.broadcasted_iota(jnp.int32, (rows, page), 0) // SB_HEADS
    c_idx = lax.broadcasted_iota(jnp.int32, (rows, page), 1)
    valid = c_idx < t_of_row
    z = _dot_nt(q_bd, kn) + bias
    sp = jnp.where(valid, _softplus(z), 0.0)
    incl = _dot(sp.astype(BF16), u_incl[:page, :page])
    w = jnp.where(valid, jnp.exp(z - incl), 0.0).astype(BF16)
    carry = jnp.sum(sp, axis=1, keepdims=True)
    o = _dot(w, vn)
    z_pages = [_dot(q_bd, kt_pages[p][0].astype(BF16)) + bias for p in range(n_pages)]
    for j in reversed(range(n_pages // 2)):
        z = jnp.concatenate([z_pages[2 * j], z_pages[2 * j + 1]], axis=1)
        sp = _softplus(z)
        incl = _dot(sp.astype(BF16), u_incl)
        w = jnp.exp(z - incl - carry).astype(BF16)
        carry = carry + jnp.sum(sp, axis=1, keepdims=True)
        o = (o + _dot_nt(w[:, :page], vt_pages[2 * j][0].astype(BF16))
             + _dot_nt(w[:, page:], vt_pages[2 * j + 1][0].astype(BF16)))
    out = jnp.concatenate(
        [jnp.sum(jnp.where(own, o[t * SB_HEADS:(t + 1) * SB_HEADS, :], 0.0), axis=0, keepdims=True)
         for t in range(t_new)], axis=0)
    out_ref[0] = out * _silu(gate_ref[0])


def _sb_sample(q, k_new, v_new, gate, bias_col, cache_kt, cache_vt, page_table):
    n_seq, t_new, width = q.shape
    n_pages = page_table.shape[1]
    page = cache_kt.shape[2]
    pt_flat = page_table.reshape(-1)
    tok = lambda b, pt: (b, 0, 0)

    def page_spec(j):
        return pl.BlockSpec((1, width, page), lambda b, pt: (pt[b * n_pages + j], 0, 0))

    grid_spec = pltpu.PrefetchScalarGridSpec(
        num_scalar_prefetch=1, grid=(n_seq,),
        in_specs=[pl.BlockSpec(bias_col.shape, lambda b, pt: (0, 0)),
                  pl.BlockSpec((1, t_new, width), tok), pl.BlockSpec((1,) + k_new.shape[1:], tok),
                  pl.BlockSpec((1,) + v_new.shape[1:], tok), pl.BlockSpec((1, t_new, width), tok)]
                 + [page_spec(j) for j in range(n_pages)] * 2,
        out_specs=pl.BlockSpec((1, t_new, width), tok))
    return pl.pallas_call(
        functools.partial(_sb_sample_body, n_pages, page, t_new), grid_spec=grid_spec,
        out_shape=jax.ShapeDtypeStruct((n_seq, t_new, width), F32),
        compiler_params=_params("parallel"), name="sb_sample")(
            pt_flat, bias_col, q, k_new, v_new, gate, *([cache_kt] * n_pages), *([cache_vt] * n_pages))


def _trunk(x, s_init, past, wts, n_seq, t_len, tm):
    (g_norm_a, w_main_a, w_lr_a, w_a2, b_a, g_head_a, w_out_a, g_kv, w_kv, g_norm_b, w_in_b, b_sb,
     w_out_b, g_final) = wts
    n_a = w_main_a.shape[0]
    n_b = w_in_b.shape[0]
    is_prompt = past is None
    h, a, w_o = x, None, None
    states = []
    for i in range(n_a):
        h, qk, v, gate, la = _gla_in_stage(h, a, w_o, g_norm_a[i], w_main_a[i], w_lr_a[i], w_a2[i], b_a[i], tm)
        if is_prompt:
            a, s_new = _gla(qk, v, la, gate, g_head_a[i], None, i, n_seq, t_len, GLA_CHUNK, 4 * GLA_CHUNK, 1)
        else:
            def padt(u):
                u = u.reshape(n_seq, t_len, u.shape[-1])
                u = jnp.pad(u, ((0, 0), (0, SAMPLE_CHUNK - t_len), (0, 0)))
                return u.reshape(n_seq * SAMPLE_CHUNK, u.shape[-1])
            a, s_new = _gla(padt(qk), padt(v), padt(la), padt(gate), g_head_a[i], s_init, i, n_seq,
                            SAMPLE_CHUNK, SAMPLE_CHUNK, SAMPLE_CHUNK, 4)
            a = a.reshape(n_seq, SAMPLE_CHUNK, -1)[:, :t_len].reshape(n_seq * t_len, -1)
        states.append(s_new)
        w_o = w_out_a[i]
    k_sh = v_sh = kb = vb = None
    for i in range(n_b):
        if i == 0:
            h, k_sh, v_sh, kb, vb, q, gate = _sb_in_stage(h, a, w_o, g_kv, w_kv, g_norm_b[i], w_in_b[i], tm)
        else:
            h, q, gate = _sb_in_stage(h, a, w_o, None, None, g_norm_b[i], w_in_b[i], tm)
        if is_prompt:
            a = _sb_prompt(q, kb, vb, gate, b_sb[i], n_seq, t_len, 512, 256)
        else:
            cache_k, cache_v, page_table = past
            width = q.shape[1]
            pad8 = lambda u: jnp.pad(u.reshape(n_seq, t_len, width), ((0, 0), (0, 8 - t_len), (0, 0)))
            bias_col = jnp.tile(b_sb[i], t_len).reshape(t_len * SB_HEADS, 1)
            a = _sb_sample(q.astype(F32).reshape(n_seq, t_len, width), pad8(k_sh), pad8(v_sh),
                           gate.reshape(n_seq, t_len, width), bias_col, cache_k, cache_v, page_table)
            a = a.reshape(n_seq * t_len, width)
        w_o = w_out_b[i]
    y = _final_stage(h, a, w_o, g_final, tm)
    return y, jnp.stack(states), k_sh, v_sh


def kernel(x_prompt, x_sample, state_gla, cache_k, cache_v, page_table, g_norm_a, w_in_a, w_alpha2, b_alpha,
           g_head_a, w_out_a, g_kv, w_kv, g_norm_b, w_in_b, b_sb, w_out_b, g_final):
    n_p, t_p, d = x_prompt.shape
    n_s, t_s, _ = x_sample.shape
    dk = w_alpha2.shape[2]
    n_main = w_in_a.shape[2] - w_alpha2.shape[1]
    row = lambda g: g[..., None, :]
    wts = (row(g_norm_a), w_in_a[:, :, :n_main].astype(BF16), w_in_a[:, :, n_main:].astype(BF16),
           w_alpha2.astype(BF16), row(b_alpha), row(g_head_a), w_out_a.astype(BF16), row(g_kv),
           w_kv.astype(BF16), row(g_norm_b), w_in_b.astype(BF16), b_sb, w_out_b.astype(BF16), row(g_final))
    del dk
    sb_shape = (SB_HEADS, w_kv.shape[1] // 2 // SB_HEADS)
    y_p, s_p, k_p, v_p = _trunk(x_prompt.reshape(n_p * t_p, d), None, None, wts, n_p, t_p, 256)
    n_phys, page = cache_k.shape[:2]
    to_t = lambda c: jnp.transpose(c, (0, 2, 3, 1)).reshape(n_phys, -1, page)
    past = (to_t(cache_k), to_t(cache_v), page_table)
    y_s, s_s, k_s, v_s = _trunk(x_sample.reshape(n_s * t_s, d), state_gla, past, wts, n_s, t_s, 256)
    return (y_p.reshape(n_p, t_p, d), y_s.reshape(n_s, t_s, d), s_p, s_s,
            k_p.reshape(n_p, t_p, *sb_shape), v_p.reshape(n_p, t_p, *sb_shape),
            k_s.reshape(n_s, t_s, *sb_shape), v_s.reshape(n_s, t_s, *sb_shape))
```

```python
import functools

import jax
import jax.numpy as jnp
from jax import lax
from jax.experimental import pallas as pl
from jax.experimental.pallas import tpu as pltpu

F32 = jnp.float32
BF16 = jnp.bfloat16

RMS_EPS = 1e-6
GLA_HEADS = 4
GLA_TAU = 16.0
GLA_CHUNK = 64
SB_HEADS = 16
LANES = 128
SAMPLE_CHUNK = 16
SB_KEY_BLOCK = 256
VMEM_LIMIT = 56 * 1024 * 1024

_NT = (((1,), (1,)), ((), ()))
_TN = (((0,), (0,)), ((), ()))


def _dot(a, b):
    return jnp.dot(a, b, preferred_element_type=F32)


def _dot_nt(a, b):
    return lax.dot_general(a, b, _NT, preferred_element_type=F32)


def _dot_tn(a, b):
    return lax.dot_general(a, b, _TN, preferred_element_type=F32)


def _split_bf16(x):
    hi = x.astype(BF16)
    lo = (x - hi.astype(F32)).astype(BF16)
    return hi, lo


def _xhat(h):
    return h * lax.rsqrt(jnp.mean(h * h, axis=-1, keepdims=True) + RMS_EPS)


def _silu(x):
    return x / (1.0 + jnp.exp(-x))


LOG2_E = 1.4426950408889634
MASKED_LOGIT = -1e30


def _softplus(z):
    return jnp.maximum(z, 0.0) + jnp.log(1.0 + jnp.exp2(jnp.abs(z) * (-LOG2_E)))


def _params(*sem):
    return pltpu.CompilerParams(dimension_semantics=sem, vmem_limit_bytes=VMEM_LIMIT)


def _row_spec(tm, n):
    return pl.BlockSpec((tm, n), lambda i: (i, 0))


def _full_spec(shape):
    return pl.BlockSpec(shape, lambda i: (0,) * len(shape))


def _resid(refs, has_resid):
    if has_resid:
        h_ref, a_ref, wo_ref = refs[:3]
        return h_ref[...] + _dot(a_ref[...].astype(BF16), wo_ref[...]), refs[3:]
    return refs[0][...], refs[1:]


def _gla_in_body(has_resid, dk, dv, *refs):
    h, refs = _resid(refs, has_resid)
    g_ref, w_ref, wlr_ref, wa2_ref, ba_ref = refs[:5]
    outs = refs[5:]
    if has_resid:
        outs[0][...] = h
        outs = outs[1:]
    qk_ref, v_ref, gate_ref, la_ref = outs
    xn = (_xhat(h) * g_ref[...]).astype(BF16)
    qk_ref[...] = _dot(xn, w_ref[:, 0:2 * dk])
    v_ref[...] = _dot(xn, w_ref[:, 2 * dk:2 * dk + dv]).astype(BF16)
    gate_ref[...] = _dot(xn, w_ref[:, 2 * dk + dv:2 * dk + 2 * dv])
    lr = _dot(xn, wlr_ref[...])
    zz = _dot(lr.astype(BF16), wa2_ref[...]) + ba_ref[...]
    la_ref[...] = (-_softplus(-zz)) * (1.0 / GLA_TAU)


def _gla_in_stage(h, a, w_o, g, w_main, w_lr, w_a2, b_a, tm):
    m, d = h.shape
    dk = w_a2.shape[1]
    dv = (w_main.shape[1] - 2 * dk) // 2
    has_resid = a is not None
    ins, specs = [h], [_row_spec(tm, d)]
    if has_resid:
        ins += [a, w_o]
        specs += [_row_spec(tm, a.shape[1]), _full_spec(w_o.shape)]
    ins += [g, w_main, w_lr, w_a2, b_a]
    specs += [_full_spec(x.shape) for x in (g, w_main, w_lr, w_a2, b_a)]
    out_shape, out_specs = [], []
    if has_resid:
        out_shape.append(jax.ShapeDtypeStruct((m, d), F32))
        out_specs.append(_row_spec(tm, d))
    out_shape += [jax.ShapeDtypeStruct((m, 2 * dk), F32), jax.ShapeDtypeStruct((m, dv), BF16),
                  jax.ShapeDtypeStruct((m, dv), F32), jax.ShapeDtypeStruct((m, dk), F32)]
    out_specs += [_row_spec(tm, 2 * dk), _row_spec(tm, dv), _row_spec(tm, dv), _row_spec(tm, dk)]
    res = pl.pallas_call(
        functools.partial(_gla_in_body, has_resid, dk, dv),
        grid=(m // tm,), in_specs=specs, out_specs=out_specs, out_shape=out_shape,
        compiler_params=_params("parallel"), name="gla_in_stage")(*ins)
    if not has_resid:
        res = [h] + list(res)
    return res


def _sb_in_body(kv_mode, width, q_scale, *refs):
    h, refs = _resid(refs, True)
    if kv_mode == "rows":
        gkv_ref, wkv_ref, gb_ref, wb_ref = refs[:4]
        h_out, k_ref, v_ref, kb_ref, vb_ref, q_ref, gate_ref = refs[4:]
    elif kv_mode == "transposed":
        gkv_ref, wkv_ref, wkt_ref, wvt_ref, gb_ref, wb_ref = refs[:6]
        h_out, kt_ref, vt_ref, ktb_ref, vb_ref, q_ref, gate_ref = refs[6:]
    else:
        gb_ref, wb_ref = refs[:2]
        h_out, q_ref, gate_ref = refs[2:]
    h_out[...] = h
    xh = _xhat(h)
    if kv_mode is not None:
        xkv = (xh * gkv_ref[...]).astype(BF16)
        v = _dot(xkv, wkv_ref[:, width:2 * width])
        vb_ref[...] = v.astype(BF16)
        if kv_mode == "rows":
            k = _dot(xkv, wkv_ref[:, 0:width])
            k_ref[...] = k
            v_ref[...] = v
            kb_ref[...] = k.astype(BF16)
        else:
            kt = _dot_nt(wkt_ref[...], xkv)
            kt_ref[0] = kt
            ktb_ref[0, 0] = kt.astype(BF16)
            vt_ref[0] = _dot_nt(wvt_ref[...], xkv)
    xb = (xh * gb_ref[...]).astype(BF16)
    q_ref[...] = (_dot(xb, wb_ref[:, 0:width]) * q_scale).astype(BF16)
    gate_ref[...] = _dot(xb, wb_ref[:, width:2 * width])


def _sb_in_stage(h, a, w_o, kv, g_b, w_b, tm, t_len=None):
    m, d = h.shape
    width = w_b.shape[1] // 2
    kv_mode = None if kv is None else ("rows" if len(kv) == 2 else "transposed")
    ins = [h, a, w_o]
    specs = [_row_spec(tm, d), _row_spec(tm, a.shape[1]), _full_spec(w_o.shape)]
    if kv is not None:
        ins += list(kv)
        specs += [_full_spec(x.shape) for x in kv]
    ins += [g_b, w_b]
    specs += [_full_spec(g_b.shape), _full_spec(w_b.shape)]
    out_shape = [jax.ShapeDtypeStruct((m, d), F32)]
    out_specs = [_row_spec(tm, d)]
    if kv_mode == "rows":
        out_shape += [jax.ShapeDtypeStruct((m, width), F32)] * 2 + [jax.ShapeDtypeStruct((m, width), BF16)] * 2
        out_specs += [_row_spec(tm, width)] * 4
    elif kv_mode == "transposed":
        nkb = t_len // tm
        n_seq = m // t_len
        t_spec = pl.BlockSpec((1, width, tm), lambda i: (i // nkb, 0, i % nkb))
        out_shape += [jax.ShapeDtypeStruct((n_seq, width, t_len), F32)] * 2
        out_shape += [jax.ShapeDtypeStruct((n_seq, nkb, width, tm), BF16), jax.ShapeDtypeStruct((m, width), BF16)]
        out_specs += [t_spec, t_spec, pl.BlockSpec((1, 1, width, tm), lambda i: (i // nkb, i % nkb, 0, 0)),
                      _row_spec(tm, width)]
    out_shape += [jax.ShapeDtypeStruct((m, width), BF16), jax.ShapeDtypeStruct((m, width), F32)]
    out_specs += [_row_spec(tm, width)] * 2
    head_dim = width // SB_HEADS
    return pl.pallas_call(
        functools.partial(_sb_in_body, kv_mode, width, head_dim ** -0.5),
        grid=(m // tm,), in_specs=specs, out_specs=out_specs, out_shape=out_shape,
        compiler_params=_params("parallel"), name="sb_in_stage")(*ins)


def _final_body(h_ref, a_ref, wo_ref, g_ref, y_ref):
    h = h_ref[...] + _dot(a_ref[...].astype(BF16), wo_ref[...])
    y_ref[...] = _xhat(h) * g_ref[...]


def _final_stage(h, a, w_o, g, tm):
    m, d = h.shape
    return pl.pallas_call(
        _final_body, grid=(m // tm,),
        in_specs=[_row_spec(tm, d), _row_spec(tm, a.shape[1]), _full_spec(w_o.shape), _full_spec(g.shape)],
        out_specs=_row_spec(tm, d), out_shape=jax.ShapeDtypeStruct((m, d), F32),
        compiler_params=_params("parallel"), name="final_stage")(h, a, w_o, g)


def _gla_body(has_s0, n_prev, ns, chunk, n_chunks, q_scale, *refs):
    if has_s0:
        s0_ref, refs = refs[0], refs[1:]
    if n_prev:
        sprev_ref, refs = refs[0], refs[1:]
    qk_ref, v_ref, la_ref, gate_ref, gh_ref, og_ref, sout_ref, st_ref = refs
    c = pl.program_id(1)
    dk = la_ref.shape[1]
    dkh = dk // GLA_HEADS
    dvh = v_ref.shape[1] // GLA_HEADS

    @pl.when(c == 0)
    def _():
        for s in range(ns):
            for h in range(GLA_HEADS):
                st_ref[s, h] = s0_ref[0, s, h].T if has_s0 else jnp.zeros((dvh, dkh), F32)

    row = lax.broadcasted_iota(jnp.int32, (chunk, chunk), 0)
    col = lax.broadcasted_iota(jnp.int32, (chunk, chunk), 1)
    causal = row >= col
    tri = jnp.where(causal, 1.0, 0.0).astype(BF16)
    for s in range(ns):
        for n in range(n_chunks):
            sl = pl.ds((s * n_chunks + n) * chunk, chunk)
            hi, lo = _split_bf16(la_ref[sl, :])
            b = _dot(tri, hi) + _dot(tri, lo)
            b_last = b[chunk - 1:chunk, :]
            q_t = (qk_ref[sl, :dk] * q_scale * jnp.exp(b)).astype(BF16)
            kk = qk_ref[sl, dk:]
            k_t = (kk * jnp.exp(-b)).astype(BF16)
            k_e = (kk * jnp.exp(b_last - b)).astype(BF16)
            decay = jnp.exp(b_last)
            for h in range(GLA_HEADS):
                kh = slice(h * dkh, (h + 1) * dkh)
                vh = slice(h * dvh, (h + 1) * dvh)
                vv = v_ref[sl, vh]
                p = jnp.where(causal, _dot_nt(q_t[:, kh], k_t[:, kh]), 0.0).astype(BF16)
                st = st_ref[s, h]
                o = _dot(p, vv) + _dot_nt(q_t[:, kh], st.astype(BF16))
                st_ref[s, h] = decay[:, kh] * st + _dot_tn(vv, k_e[:, kh])
                on = _xhat(o) * gh_ref[...]
                og_ref[sl, vh] = (on * _silu(gate_ref[sl, vh])).astype(BF16)

    @pl.when(c == pl.num_programs(1) - 1)
    def _():
        for s in range(ns):
            for h in range(GLA_HEADS):
                sout_ref[n_prev, s, h] = st_ref[s, h].T
        if n_prev:
            sout_ref[0:n_prev] = sprev_ref[...]


def _gla(qk, v, la, gate, g_head, s0, layer, s_prev, n_seq, t_len, chunk, tc, ns):
    dk = la.shape[1]
    dv = v.shape[1]
    dkh, dvh = dk // GLA_HEADS, dv // GLA_HEADS
    nct = t_len // tc
    assert ns == 1 or nct == 1
    has_s0 = s0 is not None
    rows = ns * tc
    tok = lambda b, c: (b * nct + c, 0)
    n_prev = 0 if s_prev is None else s_prev.shape[0]
    state_spec = lambda n: pl.BlockSpec((n, ns, GLA_HEADS, dkh, dvh), lambda b, c: (0, b, 0, 0, 0))
    ins, specs = [], []
    if has_s0:
        ins.append(s0)
        specs.append(pl.BlockSpec((1, ns, GLA_HEADS, dkh, dvh), lambda b, c: (layer, b, 0, 0, 0)))
    if n_prev:
        ins.append(s_prev)
        specs.append(state_spec(n_prev))
    ins += [qk, v, la, gate, g_head]
    specs += [pl.BlockSpec((rows, 2 * dk), tok), pl.BlockSpec((rows, dv), tok), pl.BlockSpec((rows, dk), tok),
              pl.BlockSpec((rows, dv), tok), pl.BlockSpec((1, dvh), lambda b, c: (0, 0))]
    return pl.pallas_call(
        functools.partial(_gla_body, has_s0, n_prev, ns, chunk, tc // chunk, dkh ** -0.5),
        grid=(n_seq // ns, nct), in_specs=specs,
        out_specs=[pl.BlockSpec((rows, dv), tok), state_spec(n_prev + 1)],
        out_shape=[jax.ShapeDtypeStruct((n_seq * t_len, dv), BF16),
                   jax.ShapeDtypeStruct((n_prev + 1, n_seq, GLA_HEADS, dkh, dvh), F32)],
        scratch_shapes=[pltpu.VMEM((ns, GLA_HEADS, dvh, dkh), F32)],
        compiler_params=_params("parallel", "arbitrary"), name="gla")(*ins)


def _upper_incl(n):
    j = lax.broadcasted_iota(jnp.int32, (n, n), 0)
    s = lax.broadcasted_iota(jnp.int32, (n, n), 1)
    return jnp.where(j >= s, 1.0, 0.0).astype(BF16)


def _sb_prompt_body(tq, tk, bias_ref, q_ref, kt_ref, v_ref, gate_ref, out_ref, u_ref, *scratch):
    qh_ref, o_ref, c_ref, mb_ref = scratch[0:2], scratch[2:4], scratch[4:6], scratch[6:8]
    z_ref = (scratch[8:10], scratch[10:12])
    w_ref = (scratch[12:14], scratch[14:16])
    pair = pl.program_id(1)
    i = pl.program_id(2)
    head_dim = LANES // 2
    lane = lax.broadcasted_iota(jnp.int32, (1, LANES), 1)
    q = q_ref[...]
    zero = jnp.zeros_like(q)
    qh_ref[0][...] = jnp.where(lane < head_dim, q, zero)
    qh_ref[1][...] = jnp.where(lane >= head_dim, q, zero)
    u_ref[...] = _upper_incl(tk)
    r = lax.broadcasted_iota(jnp.int32, (tk, tk), 0)
    s = lax.broadcasted_iota(jnp.int32, (tk, tk), 1)
    bias = [bias_ref[2 * pair + hh] for hh in range(2)]
    for hh in range(2):
        o_ref[hh][...] = jnp.zeros_like(o_ref[hh])
        c_ref[hh][...] = jnp.zeros_like(c_ref[hh])
        mb_ref[hh][...] = jnp.where(s < r, bias[hh], MASKED_LOGIT)
    assert tq == 2 * tk

    def logits(j, slot, r0=0, tri0=None):
        kt = kt_ref[0, j]
        for hh in range(2):
            d = _dot(qh_ref[hh][r0:, :], kt)
            if tri0 is None:
                z_ref[slot][hh][r0:, :] = d + bias[hh]
            else:
                z_ref[slot][hh][tri0:tri0 + tk, :] = d[tri0 - r0:tri0 - r0 + tk, :] + mb_ref[hh][...]
                if tri0 + tk < tq:
                    z_ref[slot][hh][tri0 + tk:, :] = d[tri0 - r0 + tk:, :] + bias[hh]

    def weights(slot, r0=0):
        for hh in range(2):
            sp = _softplus(z_ref[slot][hh][r0:, :])
            incl = _dot(sp.astype(BF16), u_ref[...])
            w = jnp.exp(z_ref[slot][hh][r0:, :] - incl - c_ref[hh][r0:, :])
            w_ref[slot][hh][r0:, :] = w.astype(BF16)
            c_ref[hh][r0:, :] += jnp.sum(sp, axis=1, keepdims=True)

    def pv(j, slot, r0=0):
        vs = v_ref[pl.ds(pl.multiple_of(j * tk, tk), tk), :]
        for hh in range(2):
            o_ref[hh][r0:, :] += _dot(w_ref[slot][hh][r0:, :], vs)

    top = 2 * i + 1
    logits(top, 0, tk, tk)
    weights(0, tk)
    logits(top - 1, 1, 0, 0)
    weights(1)
    pv(top, 0, tk)
    logits(jnp.maximum(top - 2, 0), 0)

    def body(p, carry):
        j0 = top - 2 * p
        weights(0)
        pv(j0 + 1, 1)
        logits(j0 - 1, 1)
        weights(1)
        pv(j0, 0)
        logits(jnp.maximum(j0 - 2, 0), 0)
        return carry

    lax.fori_loop(1, i + 1, body, 0)
    pv(0, 1)
    o = jnp.where(lane < head_dim, o_ref[0][...], o_ref[1][...])
    out_ref[...] = (o * _silu(gate_ref[...])).astype(BF16)


def _sb_prompt(q, ktb, vb, gate, bias, n_seq, t_len, tq, tk):
    m, width = q.shape
    nq = t_len // tq
    nkb = t_len // tk
    tok = lambda b, p, i, bias_ref: (b * nq + i, p)
    grid_spec = pltpu.PrefetchScalarGridSpec(
        num_scalar_prefetch=1, grid=(n_seq, width // LANES, nq),
        in_specs=[pl.BlockSpec((tq, LANES), tok),
                  pl.BlockSpec((1, nkb, LANES, tk), lambda b, p, i, bias_ref: (b, 0, p, 0)),
                  pl.BlockSpec((t_len, LANES), lambda b, p, i, bias_ref: (b, p)),
                  pl.BlockSpec((tq, LANES), tok)],
        out_specs=pl.BlockSpec((tq, LANES), tok),
        scratch_shapes=[pltpu.VMEM((tk, tk), BF16)] + [pltpu.VMEM((tq, LANES), BF16)] * 2
        + [pltpu.VMEM((tq, LANES), F32)] * 2 + [pltpu.VMEM((tq, 1), F32)] * 2 + [pltpu.VMEM((tk, tk), F32)] * 2
        + [pltpu.VMEM((tq, tk), F32)] * 4 + [pltpu.VMEM((tq, tk), BF16)] * 4)
    return pl.pallas_call(
        functools.partial(_sb_prompt_body, tq, tk), grid_spec=grid_spec,
        out_shape=jax.ShapeDtypeStruct((m, width), BF16),
        compiler_params=_params("parallel", "parallel", "arbitrary"), name="sb_prompt")(bias, q, ktb, vb, gate)


def _sb_sample_body(n_pages, page, t_new, pt_ref, bias_ref, q_ref, knew_ref, vnew_ref, gate_ref, *refs):
    kt_pages = refs[:n_pages]
    vt_pages = refs[n_pages:2 * n_pages]
    out_ref = refs[2 * n_pages]
    width = q_ref.shape[-1]
    head_dim = width // SB_HEADS
    rows = t_new * SB_HEADS
    head_of_lane = lax.broadcasted_iota(jnp.int32, (SB_HEADS, width), 1) // head_dim
    head_of_row = lax.broadcasted_iota(jnp.int32, (SB_HEADS, width), 0)
    own = head_of_lane == head_of_row
    q = q_ref[0]
    q_bd = jnp.concatenate(
        [jnp.where(own, jnp.broadcast_to(q[t:t + 1, :], (SB_HEADS, width)), 0.0) for t in range(t_new)],
        axis=0).astype(BF16)
    bias = bias_ref[...]
    blk = 2 * page
    u_incl = _upper_incl(blk)

    pad = jnp.zeros((page - knew_ref.shape[1], width), F32)
    kn = jnp.concatenate([knew_ref[0], pad], axis=0).astype(BF16)
    vn = jnp.concatenate([vnew_ref[0], pad], axis=0).astype(BF16)
    t_of_row = lax.broadcasted_iota(jnp.int32, (rows, page), 0) // SB_HEADS
    c_idx = lax.broadcasted_iota(jnp.int32, (rows, page), 1)
    valid = c_idx < t_of_row
    z = _dot_nt(q_bd, kn) + bias
    sp = jnp.where(valid, _softplus(z), 0.0)
    incl = _dot(sp.astype(BF16), u_incl[:page, :page])
    w = jnp.where(valid, jnp.exp(z - incl), 0.0).astype(BF16)
    carry = jnp.sum(sp, axis=1, keepdims=True)
    o = _dot(w, vn)
    z_pages = [_dot(q_bd, kt_pages[p][0].astype(BF16)) + bias for p in range(n_pages)]
    for j in reversed(range(n_pages // 2)):
        z = jnp.concatenate([z_pages[2 * j], z_pages[2 * j + 1]], axis=1)
        sp = _softplus(z)
        incl = _dot(sp.astype(BF16), u_incl)
        w = jnp.exp(z - incl - carry).astype(BF16)
        carry = carry + jnp.sum(sp, axis=1, keepdims=True)
        o = (o + _dot_nt(w[:, :page], vt_pages[2 * j][0].astype(BF16))
             + _dot_nt(w[:, page:], vt_pages[2 * j + 1][0].astype(BF16)))
    out = jnp.concatenate(
        [jnp.sum(jnp.where(own, o[t * SB_HEADS:(t + 1) * SB_HEADS, :], 0.0), axis=0, keepdims=True)
         for t in range(t_new)], axis=0)
    out_ref[0] = out * _silu(gate_ref[0])


def _sb_sample(q, k_new, v_new, gate, bias_col, cache_kt, cache_vt, page_table):
    n_seq, t_new, width = q.shape
    n_pages = page_table.shape[1]
    page = cache_kt.shape[2]
    pt_flat = page_table.reshape(-1)
    tok = lambda b, pt: (b, 0, 0)

    def page_spec(j):
        return pl.BlockSpec((1, width, page), lambda b, pt: (pt[b * n_pages + j], 0, 0))

    grid_spec = pltpu.PrefetchScalarGridSpec(
        num_scalar_prefetch=1, grid=(n_seq,),
        in_specs=[pl.BlockSpec(bias_col.shape, lambda b, pt: (0, 0)),
                  pl.BlockSpec((1, t_new, width), tok), pl.BlockSpec((1,) + k_new.shape[1:], tok),
                  pl.BlockSpec((1,) + v_new.shape[1:], tok), pl.BlockSpec((1, t_new, width), tok)]
                 + [page_spec(j) for j in range(n_pages)] * 2,
        out_specs=pl.BlockSpec((1, t_new, width), tok))
    return pl.pallas_call(
        functools.partial(_sb_sample_body, n_pages, page, t_new), grid_spec=grid_spec,
        out_shape=jax.ShapeDtypeStruct((n_seq, t_new, width), F32),
        compiler_params=_params("parallel"), name="sb_sample")(
            pt_flat, bias_col, q, k_new, v_new, gate, *([cache_kt] * n_pages), *([cache_vt] * n_pages))


def _trunk(x, s_init, past, wts, n_seq, t_len, tm):
    (g_norm_a, w_main_a, w_lr_a, w_a2, b_a, g_head_a, w_out_a, g_kv, w_kv, w_kt, w_vt, g_norm_b, w_in_b, b_sb,
     w_out_b, g_final) = wts
    n_a = w_main_a.shape[0]
    n_b = w_in_b.shape[0]
    is_prompt = past is None
    h, a, w_o = x, None, None
    states = None
    for i in range(n_a):
        h, qk, v, gate, la = _gla_in_stage(h, a, w_o, g_norm_a[i], w_main_a[i], w_lr_a[i], w_a2[i], b_a[i], tm)
        if is_prompt:
            a, states = _gla(qk, v, la, gate, g_head_a[i], None, i, states, n_seq, t_len, GLA_CHUNK,
                             4 * GLA_CHUNK, 1)
        else:
            def padt(u):
                u = u.reshape(n_seq, t_len, u.shape[-1])
                u = jnp.pad(u, ((0, 0), (0, SAMPLE_CHUNK - t_len), (0, 0)))
                return u.reshape(n_seq * SAMPLE_CHUNK, u.shape[-1])
            a, states = _gla(padt(qk), padt(v), padt(la), padt(gate), g_head_a[i], s_init, i, states, n_seq,
                             SAMPLE_CHUNK, SAMPLE_CHUNK, SAMPLE_CHUNK, 4)
            a = a.reshape(n_seq, SAMPLE_CHUNK, -1)[:, :t_len].reshape(n_seq * t_len, -1)
        w_o = w_out_a[i]
    k_sh = v_sh = ktb = vb = None
    for i in range(n_b):
        if i > 0:
            h, q, gate = _sb_in_stage(h, a, w_o, None, g_norm_b[i], w_in_b[i], tm)
        elif is_prompt:
            h, k_sh, v_sh, ktb, vb, q, gate = _sb_in_stage(h, a, w_o, (g_kv, w_kv, w_kt, w_vt), g_norm_b[i],
                                                           w_in_b[i], SB_KEY_BLOCK, t_len)
        else:
            h, k_sh, v_sh, _, _, q, gate = _sb_in_stage(h, a, w_o, (g_kv, w_kv), g_norm_b[i], w_in_b[i], tm)
        if is_prompt:
            a = _sb_prompt(q, ktb, vb, gate, b_sb[i], n_seq, t_len, 2 * SB_KEY_BLOCK, SB_KEY_BLOCK)
        else:
            cache_k, cache_v, page_table = past
            width = q.shape[1]
            pad8 = lambda u: jnp.pad(u.reshape(n_seq, t_len, width), ((0, 0), (0, 8 - t_len), (0, 0)))
            bias_col = jnp.tile(b_sb[i], t_len).reshape(t_len * SB_HEADS, 1)
            a = _sb_sample(q.astype(F32).reshape(n_seq, t_len, width), pad8(k_sh), pad8(v_sh),
                           gate.reshape(n_seq, t_len, width), bias_col, cache_k, cache_v, page_table)
            a = a.reshape(n_seq * t_len, width)
        w_o = w_out_b[i]
    y = _final_stage(h, a, w_o, g_final, tm)
    return y, states, k_sh, v_sh


def kernel(x_prompt, x_sample, state_gla, cache_k, cache_v, page_table, g_norm_a, w_in_a, w_alpha2, b_alpha,
           g_head_a, w_out_a, g_kv, w_kv, g_norm_b, w_in_b, b_sb, w_out_b, g_final):
    n_p, t_p, d = x_prompt.shape
    n_s, t_s, _ = x_sample.shape
    n_main = w_in_a.shape[2] - w_alpha2.shape[1]
    width = w_kv.shape[1] // 2
    sb_shape = (SB_HEADS, width // SB_HEADS)
    row = lambda g: g[..., None, :]
    wts = (row(g_norm_a), w_in_a[:, :, :n_main].astype(BF16), w_in_a[:, :, n_main:].astype(BF16),
           w_alpha2.astype(BF16), row(b_alpha), row(g_head_a), w_out_a.astype(BF16), row(g_kv),
           w_kv.astype(BF16), w_kv[:, :width].T.astype(BF16), w_kv[:, width:].T.astype(BF16),
           row(g_norm_b), w_in_b.astype(BF16), b_sb, w_out_b.astype(BF16), row(g_final))
    y_p, s_p, k_p, v_p = _trunk(x_prompt.reshape(n_p * t_p, d), None, None, wts, n_p, t_p, 256)
    n_phys, page = cache_k.shape[:2]
    to_t = lambda c: jnp.transpose(c, (0, 2, 3, 1)).reshape(n_phys, -1, page)
    past = (to_t(cache_k), to_t(cache_v), page_table)
    y_s, s_s, k_s, v_s = _trunk(x_sample.reshape(n_s * t_s, d), state_gla, past, wts, n_s, t_s, 256)
    from_t = lambda u: jnp.transpose(u.reshape(n_p, *sb_shape, t_p), (0, 3, 1, 2))
    return (y_p.reshape(n_p, t_p, d), y_s.reshape(n_s, t_s, d), s_p, s_s, from_t(k_p), from_t(v_p),
            k_s.reshape(n_s, t_s, *sb_shape), v_s.reshape(n_s, t_s, *sb_shape))
```

```python
import functools

import jax
import jax.numpy as jnp
from jax import lax
from jax.experimental import pallas as pl
from jax.experimental.pallas import tpu as pltpu

F32 = jnp.float32
BF16 = jnp.bfloat16

RMS_EPS = 1e-6
GLA_HEADS = 4
GLA_TAU = 16.0
GLA_CHUNK = 64
SB_HEADS = 16
LANES = 128
SAMPLE_CHUNK = 16
SB_KEY_BLOCK = 256
VMEM_LIMIT = 56 * 1024 * 1024

_NT = (((1,), (1,)), ((), ()))
_TN = (((0,), (0,)), ((), ()))


def _dot(a, b):
    return jnp.dot(a, b, preferred_element_type=F32)


def _dot_nt(a, b):
    return lax.dot_general(a, b, _NT, preferred_element_type=F32)


def _dot_tn(a, b):
    return lax.dot_general(a, b, _TN, preferred_element_type=F32)


def _split_bf16(x):
    hi = x.astype(BF16)
    lo = (x - hi.astype(F32)).astype(BF16)
    return hi, lo


def _xhat(h):
    return h * lax.rsqrt(jnp.mean(h * h, axis=-1, keepdims=True) + RMS_EPS)


def _silu(x):
    return x / (1.0 + jnp.exp(-x))


LOG2_E = 1.4426950408889634
MASKED_LOGIT = -1e30


def _softplus(z):
    return jnp.maximum(z, 0.0) + jnp.log(1.0 + jnp.exp2(jnp.abs(z) * (-LOG2_E)))


def _params(*sem):
    return pltpu.CompilerParams(dimension_semantics=sem, vmem_limit_bytes=VMEM_LIMIT)


def _row_spec(tm, n):
    return pl.BlockSpec((tm, n), lambda i: (i, 0))


def _full_spec(shape):
    return pl.BlockSpec(shape, lambda i: (0,) * len(shape))


def _resid(refs, has_resid):
    if has_resid:
        h_ref, a_ref, wo_ref = refs[:3]
        return h_ref[...] + _dot(a_ref[...].astype(BF16), wo_ref[...]), refs[3:]
    return refs[0][...], refs[1:]


def _gla_in_body(has_resid, dk, dv, *refs):
    h, refs = _resid(refs, has_resid)
    g_ref, w_ref, wlr_ref, wa2_ref, ba_ref = refs[:5]
    outs = refs[5:]
    if has_resid:
        outs[0][...] = h
        outs = outs[1:]
    qk_ref, v_ref, gate_ref, la_ref = outs
    xn = (_xhat(h) * g_ref[...]).astype(BF16)
    qk_ref[...] = _dot(xn, w_ref[:, 0:2 * dk])
    v_ref[...] = _dot(xn, w_ref[:, 2 * dk:2 * dk + dv]).astype(BF16)
    gate_ref[...] = _dot(xn, w_ref[:, 2 * dk + dv:2 * dk + 2 * dv])
    lr = _dot(xn, wlr_ref[...])
    zz = _dot(lr.astype(BF16), wa2_ref[...]) + ba_ref[...]
    la_ref[...] = (-_softplus(-zz)) * (1.0 / GLA_TAU)


def _gla_in_stage(h, a, w_o, g, w_main, w_lr, w_a2, b_a, tm):
    m, d = h.shape
    dk = w_a2.shape[1]
    dv = (w_main.shape[1] - 2 * dk) // 2
    has_resid = a is not None
    ins, specs = [h], [_row_spec(tm, d)]
    if has_resid:
        ins += [a, w_o]
        specs += [_row_spec(tm, a.shape[1]), _full_spec(w_o.shape)]
    ins += [g, w_main, w_lr, w_a2, b_a]
    specs += [_full_spec(x.shape) for x in (g, w_main, w_lr, w_a2, b_a)]
    out_shape, out_specs = [], []
    if has_resid:
        out_shape.append(jax.ShapeDtypeStruct((m, d), F32))
        out_specs.append(_row_spec(tm, d))
    out_shape += [jax.ShapeDtypeStruct((m, 2 * dk), F32), jax.ShapeDtypeStruct((m, dv), BF16),
                  jax.ShapeDtypeStruct((m, dv), F32), jax.ShapeDtypeStruct((m, dk), F32)]
    out_specs += [_row_spec(tm, 2 * dk), _row_spec(tm, dv), _row_spec(tm, dv), _row_spec(tm, dk)]
    res = pl.pallas_call(
        functools.partial(_gla_in_body, has_resid, dk, dv),
        grid=(m // tm,), in_specs=specs, out_specs=out_specs, out_shape=out_shape,
        compiler_params=_params("parallel"), name="gla_in_stage")(*ins)
    if not has_resid:
        res = [h] + list(res)
    return res


def _sb_in_body(kv_mode, width, q_scale, *refs):
    h, refs = _resid(refs, True)
    if kv_mode == "rows":
        gkv_ref, wkv_ref, gb_ref, wb_ref = refs[:4]
        h_out, k_ref, v_ref, kb_ref, vb_ref, q_ref, gate_ref = refs[4:]
    elif kv_mode == "transposed":
        gkv_ref, wkv_ref, wkt_ref, wvt_ref, gb_ref, wb_ref = refs[:6]
        h_out, kt_ref, vt_ref, ktb_ref, vb_ref, q_ref, gate_ref = refs[6:]
    else:
        gb_ref, wb_ref = refs[:2]
        h_out, q_ref, gate_ref = refs[2:]
    h_out[...] = h
    xh = _xhat(h)
    if kv_mode is not None:
        xkv = (xh * gkv_ref[...]).astype(BF16)
        v = _dot(xkv, wkv_ref[:, width:2 * width])
        vb_ref[...] = v.astype(BF16)
        if kv_mode == "rows":
            k = _dot(xkv, wkv_ref[:, 0:width])
            k_ref[...] = k
            v_ref[...] = v
            kb_ref[...] = k.astype(BF16)
        else:
            kt = _dot_nt(wkt_ref[...], xkv)
            kt_ref[0] = kt
            ktb_ref[0, 0] = kt.astype(BF16)
            vt_ref[0] = _dot_nt(wvt_ref[...], xkv)
    xb = (xh * gb_ref[...]).astype(BF16)
    q_ref[...] = (_dot(xb, wb_ref[:, 0:width]) * q_scale).astype(BF16)
    gate_ref[...] = _dot(xb, wb_ref[:, width:2 * width])


def _sb_in_stage(h, a, w_o, kv, g_b, w_b, tm, t_len=None):
    m, d = h.shape
    width = w_b.shape[1] // 2
    kv_mode = None if kv is None else ("rows" if len(kv) == 2 else "transposed")
    ins = [h, a, w_o]
    specs = [_row_spec(tm, d), _row_spec(tm, a.shape[1]), _full_spec(w_o.shape)]
    if kv is not None:
        ins += list(kv)
        specs += [_full_spec(x.shape) for x in kv]
    ins += [g_b, w_b]
    specs += [_full_spec(g_b.shape), _full_spec(w_b.shape)]
    out_shape = [jax.ShapeDtypeStruct((m, d), F32)]
    out_specs = [_row_spec(tm, d)]
    if kv_mode == "rows":
        out_shape += [jax.ShapeDtypeStruct((m, width), F32)] * 2 + [jax.ShapeDtypeStruct((m, width), BF16)] * 2
        out_specs += [_row_spec(tm, width)] * 4
    elif kv_mode == "transposed":
        nkb = t_len // tm
        n_seq = m // t_len
        t_spec = pl.BlockSpec((1, width, tm), lambda i: (i // nkb, 0, i % nkb))
        out_shape += [jax.ShapeDtypeStruct((n_seq, width, t_len), F32)] * 2
        out_shape += [jax.ShapeDtypeStruct((n_seq, nkb, width, tm), BF16), jax.ShapeDtypeStruct((m, width), BF16)]
        out_specs += [t_spec, t_spec, pl.BlockSpec((1, 1, width, tm), lambda i: (i // nkb, i % nkb, 0, 0)),
                      _row_spec(tm, width)]
    out_shape += [jax.ShapeDtypeStruct((m, width), BF16), jax.ShapeDtypeStruct((m, width), F32)]
    out_specs += [_row_spec(tm, width)] * 2
    head_dim = width // SB_HEADS
    return pl.pallas_call(
        functools.partial(_sb_in_body, kv_mode, width, head_dim ** -0.5),
        grid=(m // tm,), in_specs=specs, out_specs=out_specs, out_shape=out_shape,
        compiler_params=_params("parallel"), name="sb_in_stage")(*ins)


def _final_body(h_ref, a_ref, wo_ref, g_ref, y_ref):
    h = h_ref[...] + _dot(a_ref[...].astype(BF16), wo_ref[...])
    y_ref[...] = _xhat(h) * g_ref[...]


def _final_stage(h, a, w_o, g, tm):
    m, d = h.shape
    return pl.pallas_call(
        _final_body, grid=(m // tm,),
        in_specs=[_row_spec(tm, d), _row_spec(tm, a.shape[1]), _full_spec(w_o.shape), _full_spec(g.shape)],
        out_specs=_row_spec(tm, d), out_shape=jax.ShapeDtypeStruct((m, d), F32),
        compiler_params=_params("parallel"), name="final_stage")(h, a, w_o, g)


def _gla_body(has_s0, n_prev, ns, chunk, n_chunks, q_scale, *refs):
    if has_s0:
        s0_ref, refs = refs[0], refs[1:]
    if n_prev:
        sprev_ref, refs = refs[0], refs[1:]
    qk_ref, v_ref, la_ref, gate_ref, gh_ref, og_ref, sout_ref, st_ref = refs
    c = pl.program_id(1)
    dk = la_ref.shape[1]
    dkh = dk // GLA_HEADS
    dvh = v_ref.shape[1] // GLA_HEADS

    @pl.when(c == 0)
    def _():
        for s in range(ns):
            for h in range(GLA_HEADS):
                st_ref[s, h] = s0_ref[0, s, h].T if has_s0 else jnp.zeros((dvh, dkh), F32)

    row = lax.broadcasted_iota(jnp.int32, (chunk, chunk), 0)
    col = lax.broadcasted_iota(jnp.int32, (chunk, chunk), 1)
    causal = row >= col
    tri = jnp.where(causal, 1.0, 0.0).astype(BF16)
    for s in range(ns):
        for n in range(n_chunks):
            sl = pl.ds((s * n_chunks + n) * chunk, chunk)
            hi, lo = _split_bf16(la_ref[sl, :])
            b = _dot(tri, hi) + _dot(tri, lo)
            b_last = b[chunk - 1:chunk, :]
            q_t = (qk_ref[sl, :dk] * q_scale * jnp.exp(b)).astype(BF16)
            kk = qk_ref[sl, dk:]
            k_t = (kk * jnp.exp(-b)).astype(BF16)
            k_e = (kk * jnp.exp(b_last - b)).astype(BF16)
            decay = jnp.exp(b_last)
            for h in range(GLA_HEADS):
                kh = slice(h * dkh, (h + 1) * dkh)
                vh = slice(h * dvh, (h + 1) * dvh)
                vv = v_ref[sl, vh]
                p = jnp.where(causal, _dot_nt(q_t[:, kh], k_t[:, kh]), 0.0).astype(BF16)
                st = st_ref[s, h]
                o = _dot(p, vv) + _dot_nt(q_t[:, kh], st.astype(BF16))
                st_ref[s, h] = decay[:, kh] * st + _dot_tn(vv, k_e[:, kh])
                on = _xhat(o) * gh_ref[...]
                og_ref[sl, vh] = (on * _silu(gate_ref[sl, vh])).astype(BF16)

    @pl.when(c == pl.num_programs(1) - 1)
    def _():
        for s in range(ns):
            for h in range(GLA_HEADS):
                sout_ref[n_prev, s, h] = st_ref[s, h].T
        if n_prev:
            sout_ref[0:n_prev] = sprev_ref[...]


def _gla(qk, v, la, gate, g_head, s0, layer, s_prev, n_seq, t_len, chunk, tc, ns):
    dk = la.shape[1]
    dv = v.shape[1]
    dkh, dvh = dk // GLA_HEADS, dv // GLA_HEADS
    nct = t_len // tc
    assert ns == 1 or nct == 1
    has_s0 = s0 is not None
    rows = ns * tc
    tok = lambda b, c: (b * nct + c, 0)
    n_prev = 0 if s_prev is None else s_prev.shape[0]
    state_spec = lambda n: pl.BlockSpec((n, ns, GLA_HEADS, dkh, dvh), lambda b, c: (0, b, 0, 0, 0))
    ins, specs = [], []
    if has_s0:
        ins.append(s0)
        specs.append(pl.BlockSpec((1, ns, GLA_HEADS, dkh, dvh), lambda b, c: (layer, b, 0, 0, 0)))
    if n_prev:
        ins.append(s_prev)
        specs.append(state_spec(n_prev))
    ins += [qk, v, la, gate, g_head]
    specs += [pl.BlockSpec((rows, 2 * dk), tok), pl.BlockSpec((rows, dv), tok), pl.BlockSpec((rows, dk), tok),
              pl.BlockSpec((rows, dv), tok), pl.BlockSpec((1, dvh), lambda b, c: (0, 0))]
    return pl.pallas_call(
        functools.partial(_gla_body, has_s0, n_prev, ns, chunk, tc // chunk, dkh ** -0.5),
        grid=(n_seq // ns, nct), in_specs=specs,
        out_specs=[pl.BlockSpec((rows, dv), tok), state_spec(n_prev + 1)],
        out_shape=[jax.ShapeDtypeStruct((n_seq * t_len, dv), BF16),
                   jax.ShapeDtypeStruct((n_prev + 1, n_seq, GLA_HEADS, dkh, dvh), F32)],
        scratch_shapes=[pltpu.VMEM((ns, GLA_HEADS, dvh, dkh), F32)],
        compiler_params=_params("parallel", "arbitrary"), name="gla")(*ins)


def _upper_incl(n):
    j = lax.broadcasted_iota(jnp.int32, (n, n), 0)
    s = lax.broadcasted_iota(jnp.int32, (n, n), 1)
    return jnp.where(j >= s, 1.0, 0.0).astype(BF16)


def _sb_prompt_body(tq, tk, bias_ref, q_ref, kt_ref, v_ref, gate_ref, out_ref, u_ref, *scratch):
    qh_ref, o_ref, c_ref, mb_ref = scratch[0:2], scratch[2:4], scratch[4:6], scratch[6:8]
    z_ref = (scratch[8:10], scratch[10:12])
    w_ref = (scratch[12:14], scratch[14:16])
    pair = pl.program_id(1)
    i = pl.program_id(2)
    head_dim = LANES // 2
    lane = lax.broadcasted_iota(jnp.int32, (1, LANES), 1)
    q = q_ref[...]
    zero = jnp.zeros_like(q)
    qh_ref[0][...] = jnp.where(lane < head_dim, q, zero)
    qh_ref[1][...] = jnp.where(lane >= head_dim, q, zero)
    u_ref[...] = _upper_incl(tk)
    r = lax.broadcasted_iota(jnp.int32, (tk, tk), 0)
    s = lax.broadcasted_iota(jnp.int32, (tk, tk), 1)
    bias = [bias_ref[2 * pair + hh] for hh in range(2)]
    for hh in range(2):
        o_ref[hh][...] = jnp.zeros_like(o_ref[hh])
        c_ref[hh][...] = jnp.zeros_like(c_ref[hh])
        mb_ref[hh][...] = jnp.where(s < r, bias[hh], MASKED_LOGIT)
    assert tq == 2 * tk

    def logits(j, slot, r0=0, tri0=None):
        kt = kt_ref[0, j]
        for hh in range(2):
            d = _dot(qh_ref[hh][r0:, :], kt)
            if tri0 is None:
                z_ref[slot][hh][r0:, :] = d + bias[hh]
            else:
                z_ref[slot][hh][tri0:tri0 + tk, :] = d[tri0 - r0:tri0 - r0 + tk, :] + mb_ref[hh][...]
                if tri0 + tk < tq:
                    z_ref[slot][hh][tri0 + tk:, :] = d[tri0 - r0 + tk:, :] + bias[hh]

    def weights(slot, r0=0):
        for hh in range(2):
            sp = _softplus(z_ref[slot][hh][r0:, :])
            incl = _dot(sp.astype(BF16), u_ref[...])
            w = jnp.exp(z_ref[slot][hh][r0:, :] - incl - c_ref[hh][r0:, :])
            w_ref[slot][hh][r0:, :] = w.astype(BF16)
            c_ref[hh][r0:, :] += jnp.sum(sp, axis=1, keepdims=True)

    def pv(j, slot, r0=0):
        vs = v_ref[pl.ds(pl.multiple_of(j * tk, tk), tk), :]
        for hh in range(2):
            o_ref[hh][r0:, :] += _dot(w_ref[slot][hh][r0:, :], vs)

    top = 2 * i + 1
    logits(top, 0, tk, tk)
    weights(0, tk)
    logits(top - 1, 1, 0, 0)
    weights(1)
    pv(top, 0, tk)
    logits(jnp.maximum(top - 2, 0), 0)

    def body(p, carry):
        j0 = top - 2 * p
        weights(0)
        pv(j0 + 1, 1)
        logits(j0 - 1, 1)
        weights(1)
        pv(j0, 0)
        logits(jnp.maximum(j0 - 2, 0), 0)
        return carry

    lax.fori_loop(1, i + 1, body, 0)
    pv(0, 1)
    o = jnp.where(lane < head_dim, o_ref[0][...], o_ref[1][...])
    out_ref[...] = (o * _silu(gate_ref[...])).astype(BF16)


def _sb_prompt(q, ktb, vb, gate, bias, n_seq, t_len, tq, tk):
    m, width = q.shape
    nq = t_len // tq
    nkb = t_len // tk
    tok = lambda b, p, i, bias_ref: (b * nq + i, p)
    grid_spec = pltpu.PrefetchScalarGridSpec(
        num_scalar_prefetch=1, grid=(n_seq, width // LANES, nq),
        in_specs=[pl.BlockSpec((tq, LANES), tok),
                  pl.BlockSpec((1, nkb, LANES, tk), lambda b, p, i, bias_ref: (b, 0, p, 0)),
                  pl.BlockSpec((t_len, LANES), lambda b, p, i, bias_ref: (b, p)),
                  pl.BlockSpec((tq, LANES), tok)],
        out_specs=pl.BlockSpec((tq, LANES), tok),
        scratch_shapes=[pltpu.VMEM((tk, tk), BF16)] + [pltpu.VMEM((tq, LANES), BF16)] * 2
        + [pltpu.VMEM((tq, LANES), F32)] * 2 + [pltpu.VMEM((tq, 1), F32)] * 2 + [pltpu.VMEM((tk, tk), F32)] * 2
        + [pltpu.VMEM((tq, tk), F32)] * 4 + [pltpu.VMEM((tq, tk), BF16)] * 4)
    return pl.pallas_call(
        functools.partial(_sb_prompt_body, tq, tk), grid_spec=grid_spec,
        out_shape=jax.ShapeDtypeStruct((m, width), BF16),
        compiler_params=_params("parallel", "parallel", "arbitrary"), name="sb_prompt")(bias, q, ktb, vb, gate)


def _sb_sample_attend(q, gate, bias, kn, vn, kt_pages, vt_pages, page):
    t_new, width = q.shape
    n_pages = len(kt_pages)
    head_dim = width // SB_HEADS
    rows = t_new * SB_HEADS
    head_of_lane = lax.broadcasted_iota(jnp.int32, (SB_HEADS, width), 1) // head_dim
    head_of_row = lax.broadcasted_iota(jnp.int32, (SB_HEADS, width), 0)
    own = head_of_lane == head_of_row
    q_bd = jnp.concatenate(
        [jnp.where(own, jnp.broadcast_to(q[t:t + 1, :], (SB_HEADS, width)), 0.0) for t in range(t_new)],
        axis=0).astype(BF16)
    u_incl = _upper_incl(2 * page)
    t_of_row = lax.broadcasted_iota(jnp.int32, (rows, page), 0) // SB_HEADS
    c_idx = lax.broadcasted_iota(jnp.int32, (rows, page), 1)
    z_new = _dot_nt(q_bd, kn) + jnp.where(c_idx < t_of_row, bias, MASKED_LOGIT)
    z_pages = [_dot(q_bd, kt_pages[p][0].astype(BF16)) + bias for p in range(n_pages)]
    z_blocks = [jnp.concatenate([z_pages[2 * j], z_pages[2 * j + 1]], axis=1) for j in range(n_pages // 2)]
    sp_new = _softplus(z_new)
    sp_blocks = [_softplus(z) for z in z_blocks]
    incl_new = _dot(sp_new.astype(BF16), u_incl[:page, :page])
    incl_blocks = [_dot(sp.astype(BF16), u_incl) for sp in sp_blocks]
    carry = jnp.sum(sp_new, axis=1, keepdims=True)
    carries = [None] * len(z_blocks)
    for j in reversed(range(len(z_blocks))):
        carries[j] = carry
        carry = carry + jnp.sum(sp_blocks[j], axis=1, keepdims=True)
    w_new = jnp.exp(z_new - incl_new).astype(BF16)
    w_blocks = [jnp.exp(z - incl - c).astype(BF16) for z, incl, c in zip(z_blocks, incl_blocks, carries)]
    o = _dot(w_new, vn)
    for j in range(len(z_blocks)):
        o = (o + _dot_nt(w_blocks[j][:, :page], vt_pages[2 * j][0].astype(BF16))
             + _dot_nt(w_blocks[j][:, page:], vt_pages[2 * j + 1][0].astype(BF16)))
    out = jnp.concatenate(
        [jnp.sum(jnp.where(own, o[t * SB_HEADS:(t + 1) * SB_HEADS, :], 0.0), axis=0, keepdims=True)
         for t in range(t_new)], axis=0)
    return out * _silu(gate)


def _sb_sample_body(n_layers, n_pages, page, q_scale, pt_ref, bias_ref, q_ref, knew_ref, vnew_ref, gate_ref,
                    h_ref, *refs):
    layer_refs = refs[:3 * (n_layers - 1)]
    refs = refs[3 * (n_layers - 1):]
    kt_pages = refs[:n_pages]
    vt_pages = refs[n_pages:2 * n_pages]
    a_ref, hout_ref = refs[2 * n_pages:]
    width = q_ref.shape[-1]
    pad = jnp.zeros((page - knew_ref.shape[1], width), F32)
    kn = jnp.concatenate([knew_ref[0], pad], axis=0).astype(BF16)
    vn = jnp.concatenate([vnew_ref[0], pad], axis=0).astype(BF16)
    q, gate, h = q_ref[0], gate_ref[0], h_ref[0]
    for layer in range(n_layers):
        if layer > 0:
            wo_ref, g_ref, win_ref = layer_refs[3 * (layer - 1):3 * layer]
            h = h + _dot(a.astype(BF16), wo_ref[...])
            xn = (_xhat(h) * g_ref[...]).astype(BF16)
            q = (_dot(xn, win_ref[:, 0:width]) * q_scale).astype(BF16).astype(F32)
            gate = _dot(xn, win_ref[:, width:2 * width])
        a = _sb_sample_attend(q, gate, bias_ref[:, layer:layer + 1], kn, vn, kt_pages, vt_pages, page)
    a_ref[0] = a
    hout_ref[0] = h


def _sb_sample(q, k_new, v_new, gate, h, bias_cols, later_layers, cache_kt, cache_vt, page_table):
    n_seq, t_new, width = q.shape
    n_pages = page_table.shape[1]
    page = cache_kt.shape[2]
    n_layers = 1 + len(later_layers)
    pt_flat = page_table.reshape(-1)
    tok = lambda b, pt: (b, 0, 0)
    const = lambda x: pl.BlockSpec(x.shape, lambda b, pt: (0,) * x.ndim)

    def page_spec(j):
        return pl.BlockSpec((1, width, page), lambda b, pt: (pt[b * n_pages + j], 0, 0))

    weights = [x for layer in later_layers for x in layer]
    grid_spec = pltpu.PrefetchScalarGridSpec(
        num_scalar_prefetch=1, grid=(n_seq,),
        in_specs=[const(bias_cols),
                  pl.BlockSpec((1, t_new, width), tok), pl.BlockSpec((1,) + k_new.shape[1:], tok),
                  pl.BlockSpec((1,) + v_new.shape[1:], tok), pl.BlockSpec((1, t_new, width), tok),
                  pl.BlockSpec((1,) + h.shape[1:], tok)]
                 + [const(x) for x in weights] + [page_spec(j) for j in range(n_pages)] * 2,
        out_specs=[pl.BlockSpec((1, t_new, width), tok), pl.BlockSpec((1,) + h.shape[1:], tok)])
    head_dim = width // SB_HEADS
    return pl.pallas_call(
        functools.partial(_sb_sample_body, n_layers, n_pages, page, head_dim ** -0.5), grid_spec=grid_spec,
        out_shape=[jax.ShapeDtypeStruct((n_seq, t_new, width), F32), jax.ShapeDtypeStruct(h.shape, F32)],
        compiler_params=_params("parallel"), name="sb_sample")(
            pt_flat, bias_cols, q, k_new, v_new, gate, h, *weights,
            *([cache_kt] * n_pages), *([cache_vt] * n_pages))


def _trunk(x, s_init, past, wts, n_seq, t_len, tm):
    (g_norm_a, w_main_a, w_lr_a, w_a2, b_a, g_head_a, w_out_a, g_kv, w_kv, w_kt, w_vt, g_norm_b, w_in_b, b_sb,
     w_out_b, g_final) = wts
    n_a = w_main_a.shape[0]
    n_b = w_in_b.shape[0]
    is_prompt = past is None
    h, a, w_o = x, None, None
    states = None
    for i in range(n_a):
        h, qk, v, gate, la = _gla_in_stage(h, a, w_o, g_norm_a[i], w_main_a[i], w_lr_a[i], w_a2[i], b_a[i], tm)
        if is_prompt:
            a, states = _gla(qk, v, la, gate, g_head_a[i], None, i, states, n_seq, t_len, GLA_CHUNK,
                             4 * GLA_CHUNK, 1)
        else:
            def padt(u):
                u = u.reshape(n_seq, t_len, u.shape[-1])
                u = jnp.pad(u, ((0, 0), (0, SAMPLE_CHUNK - t_len), (0, 0)))
                return u.reshape(n_seq * SAMPLE_CHUNK, u.shape[-1])
            a, states = _gla(padt(qk), padt(v), padt(la), padt(gate), g_head_a[i], s_init, i, states, n_seq,
                             SAMPLE_CHUNK, SAMPLE_CHUNK, SAMPLE_CHUNK, 4)
            a = a.reshape(n_seq, SAMPLE_CHUNK, -1)[:, :t_len].reshape(n_seq * t_len, -1)
        w_o = w_out_a[i]
    if is_prompt:
        for i in range(n_b):
            if i == 0:
                h, k_sh, v_sh, ktb, vb, q, gate = _sb_in_stage(h, a, w_o, (g_kv, w_kv, w_kt, w_vt), g_norm_b[i],
                                                               w_in_b[i], SB_KEY_BLOCK, t_len)
            else:
                h, q, gate = _sb_in_stage(h, a, w_o, None, g_norm_b[i], w_in_b[i], tm)
            a = _sb_prompt(q, ktb, vb, gate, b_sb[i], n_seq, t_len, 2 * SB_KEY_BLOCK, SB_KEY_BLOCK)
            w_o = w_out_b[i]
    else:
        cache_k, cache_v, page_table = past
        h, k_sh, v_sh, _, _, q, gate = _sb_in_stage(h, a, w_o, (g_kv, w_kv), g_norm_b[0], w_in_b[0], tm)
        width = q.shape[1]
        seq3 = lambda u: u.reshape(n_seq, t_len, u.shape[-1])
        pad8 = lambda u: jnp.pad(seq3(u), ((0, 0), (0, 8 - t_len), (0, 0)))
        bias_cols = jnp.stack([jnp.tile(b_sb[i], t_len) for i in range(n_b)], axis=1)
        later = [(w_out_b[i - 1], g_norm_b[i], w_in_b[i]) for i in range(1, n_b)]
        a, h = _sb_sample(seq3(q.astype(F32)), pad8(k_sh), pad8(v_sh), seq3(gate), seq3(h), bias_cols, later,
                          cache_k, cache_v, page_table)
        a, h = a.reshape(n_seq * t_len, width), h.reshape(n_seq * t_len, -1)
        w_o = w_out_b[n_b - 1]
    y = _final_stage(h, a, w_o, g_final, tm)
    return y, states, k_sh, v_sh


def kernel(x_prompt, x_sample, state_gla, cache_k, cache_v, page_table, g_norm_a, w_in_a, w_alpha2, b_alpha,
           g_head_a, w_out_a, g_kv, w_kv, g_norm_b, w_in_b, b_sb, w_out_b, g_final):
    n_p, t_p, d = x_prompt.shape
    n_s, t_s, _ = x_sample.shape
    n_main = w_in_a.shape[2] - w_alpha2.shape[1]
    width = w_kv.shape[1] // 2
    sb_shape = (SB_HEADS, width // SB_HEADS)
    row = lambda g: g[..., None, :]
    wts = (row(g_norm_a), w_in_a[:, :, :n_main].astype(BF16), w_in_a[:, :, n_main:].astype(BF16),
           w_alpha2.astype(BF16), row(b_alpha), row(g_head_a), w_out_a.astype(BF16), row(g_kv),
           w_kv.astype(BF16), w_kv[:, :width].T.astype(BF16), w_kv[:, width:].T.astype(BF16),
           row(g_norm_b), w_in_b.astype(BF16), b_sb, w_out_b.astype(BF16), row(g_final))
    y_p, s_p, k_p, v_p = _trunk(x_prompt.reshape(n_p * t_p, d), None, None, wts, n_p, t_p, 256)
    n_phys, page = cache_k.shape[:2]
    to_t = lambda c: jnp.transpose(c, (0, 2, 3, 1)).reshape(n_phys, -1, page)
    past = (to_t(cache_k), to_t(cache_v), page_table)
    y_s, s_s, k_s, v_s = _trunk(x_sample.reshape(n_s * t_s, d), state_gla, past, wts, n_s, t_s, 256)
    from_t = lambda u: jnp.transpose(u.reshape(n_p, *sb_shape, t_p), (0, 3, 1, 2))
    return (y_p.reshape(n_p, t_p, d), y_s.reshape(n_s, t_s, d), s_p, s_s, from_t(k_p), from_t(v_p),
            k_s.reshape(n_s, t_s, *sb_shape), v_s.reshape(n_s, t_s, *sb_shape))
```

```python
import functools

import jax
import jax.numpy as jnp
from jax import lax
from jax.experimental import pallas as pl
from jax.experimental.pallas import tpu as pltpu

F32 = jnp.float32
BF16 = jnp.bfloat16

RMS_EPS = 1e-6
GLA_HEADS = 4
GLA_TAU = 16.0
GLA_CHUNK = 64
SB_HEADS = 16
LANES = 128
SAMPLE_CHUNK = 16
SB_KEY_BLOCK = 256
VMEM_LIMIT = 56 * 1024 * 1024

_NT = (((1,), (1,)), ((), ()))
_TN = (((0,), (0,)), ((), ()))


def _dot(a, b):
    return jnp.dot(a, b, preferred_element_type=F32)


def _dot_nt(a, b):
    return lax.dot_general(a, b, _NT, preferred_element_type=F32)


def _dot_tn(a, b):
    return lax.dot_general(a, b, _TN, preferred_element_type=F32)


def _split_bf16(x):
    hi = x.astype(BF16)
    lo = (x - hi.astype(F32)).astype(BF16)
    return hi, lo


def _xhat(h):
    return h * lax.rsqrt(jnp.mean(h * h, axis=-1, keepdims=True) + RMS_EPS)


def _silu(x):
    return x / (1.0 + jnp.exp(-x))


LOG2_E = 1.4426950408889634
MASKED_LOGIT = -1e30


def _softplus(z):
    return jnp.maximum(z, 0.0) + jnp.log(1.0 + jnp.exp2(jnp.abs(z) * (-LOG2_E)))


def _params(*sem):
    return pltpu.CompilerParams(dimension_semantics=sem, vmem_limit_bytes=VMEM_LIMIT)


def _row_spec(tm, n):
    return pl.BlockSpec((tm, n), lambda i: (i, 0))


def _full_spec(shape):
    return pl.BlockSpec(shape, lambda i: (0,) * len(shape))


def _resid(refs, has_resid):
    if has_resid:
        h_ref, a_ref, wo_ref = refs[:3]
        return h_ref[...] + _dot(a_ref[...].astype(BF16), wo_ref[...]), refs[3:]
    return refs[0][...], refs[1:]


def _gla_in_body(has_resid, dk, dv, *refs):
    h, refs = _resid(refs, has_resid)
    g_ref, w_ref, wlr_ref, wa2_ref, ba_ref = refs[:5]
    outs = refs[5:]
    if has_resid:
        outs[0][...] = h
        outs = outs[1:]
    qk_ref, v_ref, gate_ref, la_ref = outs
    xn = (_xhat(h) * g_ref[...]).astype(BF16)
    qk_ref[...] = _dot(xn, w_ref[:, 0:2 * dk])
    v_ref[...] = _dot(xn, w_ref[:, 2 * dk:2 * dk + dv]).astype(BF16)
    gate_ref[...] = _dot(xn, w_ref[:, 2 * dk + dv:2 * dk + 2 * dv])
    lr = _dot(xn, wlr_ref[...])
    zz = _dot(lr.astype(BF16), wa2_ref[...]) + ba_ref[...]
    la_ref[...] = (-_softplus(-zz)) * (1.0 / GLA_TAU)


def _gla_in_stage(h, a, w_o, g, w_main, w_lr, w_a2, b_a, tm):
    m, d = h.shape
    dk = w_a2.shape[1]
    dv = (w_main.shape[1] - 2 * dk) // 2
    has_resid = a is not None
    ins, specs = [h], [_row_spec(tm, d)]
    if has_resid:
        ins += [a, w_o]
        specs += [_row_spec(tm, a.shape[1]), _full_spec(w_o.shape)]
    ins += [g, w_main, w_lr, w_a2, b_a]
    specs += [_full_spec(x.shape) for x in (g, w_main, w_lr, w_a2, b_a)]
    out_shape, out_specs = [], []
    if has_resid:
        out_shape.append(jax.ShapeDtypeStruct((m, d), F32))
        out_specs.append(_row_spec(tm, d))
    out_shape += [jax.ShapeDtypeStruct((m, 2 * dk), F32), jax.ShapeDtypeStruct((m, dv), BF16),
                  jax.ShapeDtypeStruct((m, dv), F32), jax.ShapeDtypeStruct((m, dk), F32)]
    out_specs += [_row_spec(tm, 2 * dk), _row_spec(tm, dv), _row_spec(tm, dv), _row_spec(tm, dk)]
    res = pl.pallas_call(
        functools.partial(_gla_in_body, has_resid, dk, dv),
        grid=(m // tm,), in_specs=specs, out_specs=out_specs, out_shape=out_shape,
        compiler_params=_params("parallel"), name="gla_in_stage")(*ins)
    if not has_resid:
        res = [h] + list(res)
    return res


def _sb_in_body(kv_mode, width, q_scale, *refs):
    h, refs = _resid(refs, True)
    if kv_mode == "rows":
        gkv_ref, wkv_ref, gb_ref, wb_ref = refs[:4]
        h_out, k_ref, v_ref, kb_ref, vb_ref, q_ref, gate_ref = refs[4:]
    elif kv_mode == "transposed":
        gkv_ref, wkv_ref, wkt_ref, wvt_ref, gb_ref, wb_ref = refs[:6]
        h_out, kt_ref, vt_ref, ktb_ref, vb_ref, q_ref, gate_ref = refs[6:]
    else:
        gb_ref, wb_ref = refs[:2]
        h_out, q_ref, gate_ref = refs[2:]
    h_out[...] = h
    xh = _xhat(h)
    if kv_mode is not None:
        xkv = (xh * gkv_ref[...]).astype(BF16)
        v = _dot(xkv, wkv_ref[:, width:2 * width])
        vb_ref[...] = v.astype(BF16)
        if kv_mode == "rows":
            k = _dot(xkv, wkv_ref[:, 0:width])
            k_ref[...] = k
            v_ref[...] = v
            kb_ref[...] = k.astype(BF16)
        else:
            kt = _dot_nt(wkt_ref[...], xkv)
            kt_ref[0] = kt
            ktb_ref[0, 0] = kt.astype(BF16)
            vt_ref[0] = _dot_nt(wvt_ref[...], xkv)
    xb = (xh * gb_ref[...]).astype(BF16)
    q_ref[...] = (_dot(xb, wb_ref[:, 0:width]) * q_scale).astype(BF16)
    gate_ref[...] = _dot(xb, wb_ref[:, width:2 * width])


def _sb_in_stage(h, a, w_o, kv, g_b, w_b, tm, t_len=None):
    m, d = h.shape
    width = w_b.shape[1] // 2
    kv_mode = None if kv is None else ("rows" if len(kv) == 2 else "transposed")
    ins = [h, a, w_o]
    specs = [_row_spec(tm, d), _row_spec(tm, a.shape[1]), _full_spec(w_o.shape)]
    if kv is not None:
        ins += list(kv)
        specs += [_full_spec(x.shape) for x in kv]
    ins += [g_b, w_b]
    specs += [_full_spec(g_b.shape), _full_spec(w_b.shape)]
    out_shape = [jax.ShapeDtypeStruct((m, d), F32)]
    out_specs = [_row_spec(tm, d)]
    if kv_mode == "rows":
        out_shape += [jax.ShapeDtypeStruct((m, width), F32)] * 2 + [jax.ShapeDtypeStruct((m, width), BF16)] * 2
        out_specs += [_row_spec(tm, width)] * 4
    elif kv_mode == "transposed":
        nkb = t_len // tm
        n_seq = m // t_len
        t_spec = pl.BlockSpec((1, width, tm), lambda i: (i // nkb, 0, i % nkb))
        out_shape += [jax.ShapeDtypeStruct((n_seq, width, t_len), F32)] * 2
        out_shape += [jax.ShapeDtypeStruct((n_seq, nkb, width, tm), BF16), jax.ShapeDtypeStruct((m, width), BF16)]
        out_specs += [t_spec, t_spec, pl.BlockSpec((1, 1, width, tm), lambda i: (i // nkb, i % nkb, 0, 0)),
                      _row_spec(tm, width)]
    out_shape += [jax.ShapeDtypeStruct((m, width), BF16), jax.ShapeDtypeStruct((m, width), F32)]
    out_specs += [_row_spec(tm, width)] * 2
    head_dim = width // SB_HEADS
    return pl.pallas_call(
        functools.partial(_sb_in_body, kv_mode, width, head_dim ** -0.5),
        grid=(m // tm,), in_specs=specs, out_specs=out_specs, out_shape=out_shape,
        compiler_params=_params("parallel"), name="sb_in_stage")(*ins)


def _final_body(h_ref, a_ref, wo_ref, g_ref, y_ref):
    h = h_ref[...] + _dot(a_ref[...].astype(BF16), wo_ref[...])
    y_ref[...] = _xhat(h) * g_ref[...]


def _final_stage(h, a, w_o, g, tm):
    m, d = h.shape
    return pl.pallas_call(
        _final_body, grid=(m // tm,),
        in_specs=[_row_spec(tm, d), _row_spec(tm, a.shape[1]), _full_spec(w_o.shape), _full_spec(g.shape)],
        out_specs=_row_spec(tm, d), out_shape=jax.ShapeDtypeStruct((m, d), F32),
        compiler_params=_params("parallel"), name="final_stage")(h, a, w_o, g)


def _gla_body(has_s0, n_prev, ns, chunk, n_chunks, q_scale, *refs):
    if has_s0:
        s0_ref, refs = refs[0], refs[1:]
    if n_prev:
        sprev_ref, refs = refs[0], refs[1:]
    qk_ref, v_ref, la_ref, gate_ref, gh_ref, og_ref, sout_ref, st_ref = refs
    c = pl.program_id(1)
    dk = la_ref.shape[-1]
    dkh = dk // GLA_HEADS
    dvh = v_ref.shape[-1] // GLA_HEADS

    @pl.when(c == 0)
    def _():
        for s in range(ns):
            for h in range(GLA_HEADS):
                st_ref[s, h] = s0_ref[0, s, h].T if has_s0 else jnp.zeros((dvh, dkh), F32)

    row = lax.broadcasted_iota(jnp.int32, (chunk, chunk), 0)
    col = lax.broadcasted_iota(jnp.int32, (chunk, chunk), 1)
    causal = row >= col
    tri = jnp.where(causal, 1.0, 0.0).astype(BF16)
    for s in range(ns):
        for n in range(n_chunks):
            sl = pl.ds(n * chunk, chunk)
            hi, lo = _split_bf16(la_ref[0, s, sl, :])
            b = _dot(tri, hi) + _dot(tri, lo)
            b_last = b[chunk - 1:chunk, :]
            q_t = (qk_ref[0, s, sl, :dk] * q_scale * jnp.exp(b)).astype(BF16)
            kk = qk_ref[0, s, sl, dk:]
            k_t = (kk * jnp.exp(-b)).astype(BF16)
            k_e = (kk * jnp.exp(b_last - b)).astype(BF16)
            decay = jnp.exp(b_last)
            for h in range(GLA_HEADS):
                kh = slice(h * dkh, (h + 1) * dkh)
                vh = slice(h * dvh, (h + 1) * dvh)
                vv = v_ref[0, s, sl, vh]
                p = jnp.where(causal, _dot_nt(q_t[:, kh], k_t[:, kh]), 0.0).astype(BF16)
                st = st_ref[s, h]
                o = _dot(p, vv) + _dot_nt(q_t[:, kh], st.astype(BF16))
                st_ref[s, h] = decay[:, kh] * st + _dot_tn(vv, k_e[:, kh])
                on = _xhat(o) * gh_ref[...]
                og_ref[0, s, sl, vh] = (on * _silu(gate_ref[0, s, sl, vh])).astype(BF16)

    @pl.when(c == pl.num_programs(1) - 1)
    def _():
        for s in range(ns):
            for h in range(GLA_HEADS):
                sout_ref[n_prev, s, h] = st_ref[s, h].T
        if n_prev:
            sout_ref[0:n_prev] = sprev_ref[...]


def _gla(qk, v, la, gate, g_head, s0, layer, s_prev, n_seq, t_len, chunk, tc, ns):
    dk = la.shape[1]
    dv = v.shape[1]
    dkh, dvh = dk // GLA_HEADS, dv // GLA_HEADS
    nct = t_len // tc
    has_s0 = s0 is not None
    seqs = lambda u: u.reshape(n_seq // ns, ns, t_len, u.shape[-1])
    tok = lambda width: pl.BlockSpec((1, ns, tc, width), lambda b, c: (b, 0, c, 0))
    n_prev = 0 if s_prev is None else s_prev.shape[0]
    state_spec = lambda n: pl.BlockSpec((n, ns, GLA_HEADS, dkh, dvh), lambda b, c: (0, b, 0, 0, 0))
    ins, specs = [], []
    if has_s0:
        ins.append(s0)
        specs.append(pl.BlockSpec((1, ns, GLA_HEADS, dkh, dvh), lambda b, c: (layer, b, 0, 0, 0)))
    if n_prev:
        ins.append(s_prev)
        specs.append(state_spec(n_prev))
    ins += [seqs(qk), seqs(v), seqs(la), seqs(gate), g_head]
    specs += [tok(2 * dk), tok(dv), tok(dk), tok(dv), pl.BlockSpec((1, dvh), lambda b, c: (0, 0))]
    og, states = pl.pallas_call(
        functools.partial(_gla_body, has_s0, n_prev, ns, chunk, tc // chunk, dkh ** -0.5),
        grid=(n_seq // ns, nct), in_specs=specs,
        out_specs=[tok(dv), state_spec(n_prev + 1)],
        out_shape=[jax.ShapeDtypeStruct((n_seq // ns, ns, t_len, dv), BF16),
                   jax.ShapeDtypeStruct((n_prev + 1, n_seq, GLA_HEADS, dkh, dvh), F32)],
        scratch_shapes=[pltpu.VMEM((ns, GLA_HEADS, dvh, dkh), F32)],
        compiler_params=_params("parallel", "arbitrary"), name="gla")(*ins)
    return og.reshape(n_seq * t_len, dv), states


def _upper_incl(n):
    j = lax.broadcasted_iota(jnp.int32, (n, n), 0)
    s = lax.broadcasted_iota(jnp.int32, (n, n), 1)
    return jnp.where(j >= s, 1.0, 0.0).astype(BF16)


def _sb_prompt_body(tq, tk, bias_ref, q_ref, kt_ref, v_ref, gate_ref, out_ref, u_ref, *scratch):
    qh_ref, o_ref, c_ref, mb_ref = scratch[0:2], scratch[2:4], scratch[4:6], scratch[6:8]
    z_ref = (scratch[8:10], scratch[10:12])
    w_ref = (scratch[12:14], scratch[14:16])
    pair = pl.program_id(1)
    i = pl.program_id(2)
    head_dim = LANES // 2
    lane = lax.broadcasted_iota(jnp.int32, (1, LANES), 1)
    q = q_ref[...]
    zero = jnp.zeros_like(q)
    qh_ref[0][...] = jnp.where(lane < head_dim, q, zero)
    qh_ref[1][...] = jnp.where(lane >= head_dim, q, zero)
    u_ref[...] = _upper_incl(tk)
    r = lax.broadcasted_iota(jnp.int32, (tk, tk), 0)
    s = lax.broadcasted_iota(jnp.int32, (tk, tk), 1)
    bias = [bias_ref[2 * pair + hh] for hh in range(2)]
    for hh in range(2):
        o_ref[hh][...] = jnp.zeros_like(o_ref[hh])
        c_ref[hh][...] = jnp.zeros_like(c_ref[hh])
        mb_ref[hh][...] = jnp.where(s < r, bias[hh], MASKED_LOGIT)
    assert tq == 2 * tk

    def logits(j, slot, r0=0, tri0=None):
        kt = kt_ref[0, j]
        for hh in range(2):
            d = _dot(qh_ref[hh][r0:, :], kt)
            if tri0 is None:
                z_ref[slot][hh][r0:, :] = d + bias[hh]
            else:
                z_ref[slot][hh][tri0:tri0 + tk, :] = d[tri0 - r0:tri0 - r0 + tk, :] + mb_ref[hh][...]
                if tri0 + tk < tq:
                    z_ref[slot][hh][tri0 + tk:, :] = d[tri0 - r0 + tk:, :] + bias[hh]

    def suffix_sums(slot, r0=0):
        return [_dot(_softplus(z_ref[slot][hh][r0:, :]).astype(BF16), u_ref[...]) for hh in range(2)]

    def weights(slot, incls, r0=0):
        for hh in range(2):
            w = jnp.exp(z_ref[slot][hh][r0:, :] - incls[hh] - c_ref[hh][r0:, :])
            w_ref[slot][hh][r0:, :] = w.astype(BF16)
            c_ref[hh][r0:, :] += incls[hh][:, 0:1]

    def pv(j, slot, r0=0):
        vs = v_ref[pl.ds(pl.multiple_of(j * tk, tk), tk), :]
        for hh in range(2):
            o_ref[hh][r0:, :] += _dot(w_ref[slot][hh][r0:, :], vs)

    top = 2 * i + 1
    logits(top, 0, tk, tk)
    logits(top - 1, 1, 0, 0)
    weights(0, suffix_sums(0, tk), tk)
    pv(top, 0, tk)
    logits(jnp.maximum(top - 2, 0), 0)
    weights(1, suffix_sums(1))

    def body(p, carry):
        j0 = top - 2 * p
        pv(j0 + 1, 1)
        logits(j0 - 1, 1)
        weights(0, suffix_sums(0))
        pv(j0, 0)
        logits(jnp.maximum(j0 - 2, 0), 0)
        weights(1, suffix_sums(1))
        return carry

    lax.fori_loop(1, i + 1, body, 0)
    pv(0, 1)
    o = jnp.where(lane < head_dim, o_ref[0][...], o_ref[1][...])
    out_ref[...] = (o * _silu(gate_ref[...])).astype(BF16)


def _sb_prompt(q, ktb, vb, gate, bias, n_seq, t_len, tq, tk):
    m, width = q.shape
    nq = t_len // tq
    nkb = t_len // tk
    tok = lambda b, p, i, bias_ref: (b * nq + i, p)
    grid_spec = pltpu.PrefetchScalarGridSpec(
        num_scalar_prefetch=1, grid=(n_seq, width // LANES, nq),
        in_specs=[pl.BlockSpec((tq, LANES), tok),
                  pl.BlockSpec((1, nkb, LANES, tk), lambda b, p, i, bias_ref: (b, 0, p, 0)),
                  pl.BlockSpec((t_len, LANES), lambda b, p, i, bias_ref: (b, p)),
                  pl.BlockSpec((tq, LANES), tok)],
        out_specs=pl.BlockSpec((tq, LANES), tok),
        scratch_shapes=[pltpu.VMEM((tk, tk), BF16)] + [pltpu.VMEM((tq, LANES), BF16)] * 2
        + [pltpu.VMEM((tq, LANES), F32)] * 2 + [pltpu.VMEM((tq, 1), F32)] * 2 + [pltpu.VMEM((tk, tk), F32)] * 2
        + [pltpu.VMEM((tq, tk), F32)] * 4 + [pltpu.VMEM((tq, tk), BF16)] * 4)
    return pl.pallas_call(
        functools.partial(_sb_prompt_body, tq, tk), grid_spec=grid_spec,
        out_shape=jax.ShapeDtypeStruct((m, width), BF16),
        compiler_params=_params("parallel", "parallel", "arbitrary"), name="sb_prompt")(bias, q, ktb, vb, gate)


def _sb_sample_attend(q, gate, bias, kn, vn, kt_pages, vt_pages, page):
    t_new, width = q.shape
    n_pages = len(kt_pages)
    head_dim = width // SB_HEADS
    rows = t_new * SB_HEADS
    head_of_lane = lax.broadcasted_iota(jnp.int32, (SB_HEADS, width), 1) // head_dim
    head_of_row = lax.broadcasted_iota(jnp.int32, (SB_HEADS, width), 0)
    own = head_of_lane == head_of_row
    q_bd = jnp.concatenate(
        [jnp.where(own, jnp.broadcast_to(q[t:t + 1, :], (SB_HEADS, width)), 0.0) for t in range(t_new)],
        axis=0).astype(BF16)
    u_incl = _upper_incl(2 * page)
    t_of_row = lax.broadcasted_iota(jnp.int32, (rows, page), 0) // SB_HEADS
    c_idx = lax.broadcasted_iota(jnp.int32, (rows, page), 1)
    z_new = _dot_nt(q_bd, kn) + jnp.where(c_idx < t_of_row, bias, MASKED_LOGIT)
    z_pages = [_dot(q_bd, kt_pages[p][0].astype(BF16)) + bias for p in range(n_pages)]
    z_blocks = [jnp.concatenate([z_pages[2 * j], z_pages[2 * j + 1]], axis=1) for j in range(n_pages // 2)]
    sp_new = _softplus(z_new)
    sp_blocks = [_softplus(z) for z in z_blocks]
    incl_new = _dot(sp_new.astype(BF16), u_incl[:page, :page])
    incl_blocks = [_dot(sp.astype(BF16), u_incl) for sp in sp_blocks]
    carry = jnp.sum(sp_new, axis=1, keepdims=True)
    carries = [None] * len(z_blocks)
    for j in reversed(range(len(z_blocks))):
        carries[j] = carry
        carry = carry + jnp.sum(sp_blocks[j], axis=1, keepdims=True)
    w_new = jnp.exp(z_new - incl_new).astype(BF16)
    w_blocks = [jnp.exp(z - incl - c).astype(BF16) for z, incl, c in zip(z_blocks, incl_blocks, carries)]
    o = _dot(w_new, vn)
    for j in range(len(z_blocks)):
        o = (o + _dot_nt(w_blocks[j][:, :page], vt_pages[2 * j][0].astype(BF16))
             + _dot_nt(w_blocks[j][:, page:], vt_pages[2 * j + 1][0].astype(BF16)))
    out = jnp.concatenate(
        [jnp.sum(jnp.where(own, o[t * SB_HEADS:(t + 1) * SB_HEADS, :], 0.0), axis=0, keepdims=True)
         for t in range(t_new)], axis=0)
    return out * _silu(gate)


def _sb_sample_body(n_layers, n_pages, page, q_scale, pt_ref, bias_ref, q_ref, knew_ref, vnew_ref, gate_ref,
                    h_ref, *refs):
    layer_refs = refs[:3 * (n_layers - 1)]
    refs = refs[3 * (n_layers - 1):]
    kt_pages = refs[:n_pages]
    vt_pages = refs[n_pages:2 * n_pages]
    a_ref, hout_ref = refs[2 * n_pages:]
    width = q_ref.shape[-1]
    pad = jnp.zeros((page - knew_ref.shape[1], width), F32)
    kn = jnp.concatenate([knew_ref[0], pad], axis=0).astype(BF16)
    vn = jnp.concatenate([vnew_ref[0], pad], axis=0).astype(BF16)
    q, gate, h = q_ref[0], gate_ref[0], h_ref[0]
    for layer in range(n_layers):
        if layer > 0:
            wo_ref, g_ref, win_ref = layer_refs[3 * (layer - 1):3 * layer]
            h = h + _dot(a.astype(BF16), wo_ref[...])
            xn = (_xhat(h) * g_ref[...]).astype(BF16)
            q = (_dot(xn, win_ref[:, 0:width]) * q_scale).astype(BF16).astype(F32)
            gate = _dot(xn, win_ref[:, width:2 * width])
        a = _sb_sample_attend(q, gate, bias_ref[:, layer:layer + 1], kn, vn, kt_pages, vt_pages, page)
    a_ref[0] = a
    hout_ref[0] = h


def _sb_sample(q, k_new, v_new, gate, h, bias_cols, later_layers, cache_kt, cache_vt, page_table):
    n_seq, t_new, width = q.shape
    n_pages = page_table.shape[1]
    page = cache_kt.shape[2]
    n_layers = 1 + len(later_layers)
    pt_flat = page_table.reshape(-1)
    tok = lambda b, pt: (b, 0, 0)
    const = lambda x: pl.BlockSpec(x.shape, lambda b, pt: (0,) * x.ndim)

    def page_spec(j):
        return pl.BlockSpec((1, width, page), lambda b, pt: (pt[b * n_pages + j], 0, 0))

    weights = [x for layer in later_layers for x in layer]
    grid_spec = pltpu.PrefetchScalarGridSpec(
        num_scalar_prefetch=1, grid=(n_seq,),
        in_specs=[const(bias_cols),
                  pl.BlockSpec((1, t_new, width), tok), pl.BlockSpec((1,) + k_new.shape[1:], tok),
                  pl.BlockSpec((1,) + v_new.shape[1:], tok), pl.BlockSpec((1, t_new, width), tok),
                  pl.BlockSpec((1,) + h.shape[1:], tok)]
                 + [const(x) for x in weights] + [page_spec(j) for j in range(n_pages)] * 2,
        out_specs=[pl.BlockSpec((1, t_new, width), tok), pl.BlockSpec((1,) + h.shape[1:], tok)])
    head_dim = width // SB_HEADS
    return pl.pallas_call(
        functools.partial(_sb_sample_body, n_layers, n_pages, page, head_dim ** -0.5), grid_spec=grid_spec,
        out_shape=[jax.ShapeDtypeStruct((n_seq, t_new, width), F32), jax.ShapeDtypeStruct(h.shape, F32)],
        compiler_params=_params("parallel"), name="sb_sample")(
            pt_flat, bias_cols, q, k_new, v_new, gate, h, *weights,
            *([cache_kt] * n_pages), *([cache_vt] * n_pages))


def _trunk(x, s_init, past, wts, n_seq, t_len, tm):
    (g_norm_a, w_main_a, w_lr_a, w_a2, b_a, g_head_a, w_out_a, g_kv, w_kv, w_kt, w_vt, g_norm_b, w_in_b, b_sb,
     w_out_b, g_final) = wts
    n_a = w_main_a.shape[0]
    n_b = w_in_b.shape[0]
    is_prompt = past is None
    h, a, w_o = x, None, None
    states = None
    for i in range(n_a):
        h, qk, v, gate, la = _gla_in_stage(h, a, w_o, g_norm_a[i], w_main_a[i], w_lr_a[i], w_a2[i], b_a[i], tm)
        if is_prompt:
            a, states = _gla(qk, v, la, gate, g_head_a[i], None, i, states, n_seq, t_len, GLA_CHUNK,
                             4 * GLA_CHUNK, 1)
        else:
            def padt(u):
                u = u.reshape(n_seq, t_len, u.shape[-1])
                u = jnp.pad(u, ((0, 0), (0, SAMPLE_CHUNK - t_len), (0, 0)))
                return u.reshape(n_seq * SAMPLE_CHUNK, u.shape[-1])
            a, states = _gla(padt(qk), padt(v), padt(la), padt(gate), g_head_a[i], s_init, i, states, n_seq,
                             SAMPLE_CHUNK, SAMPLE_CHUNK, SAMPLE_CHUNK, 4)
            a = a.reshape(n_seq, SAMPLE_CHUNK, -1)[:, :t_len].reshape(n_seq * t_len, -1)
        w_o = w_out_a[i]
    if is_prompt:
        for i in range(n_b):
            if i == 0:
                h, k_sh, v_sh, ktb, vb, q, gate = _sb_in_stage(h, a, w_o, (g_kv, w_kv, w_kt, w_vt), g_norm_b[i],
                                                               w_in_b[i], SB_KEY_BLOCK, t_len)
            else:
                h, q, gate = _sb_in_stage(h, a, w_o, None, g_norm_b[i], w_in_b[i], tm)
            a = _sb_prompt(q, ktb, vb, gate, b_sb[i], n_seq, t_len, 2 * SB_KEY_BLOCK, SB_KEY_BLOCK)
            w_o = w_out_b[i]
    else:
        cache_k, cache_v, page_table = past
        h, k_sh, v_sh, _, _, q, gate = _sb_in_stage(h, a, w_o, (g_kv, w_kv), g_norm_b[0], w_in_b[0], tm)
        width = q.shape[1]
        seq3 = lambda u: u.reshape(n_seq, t_len, u.shape[-1])
        pad8 = lambda u: jnp.pad(seq3(u), ((0, 0), (0, 8 - t_len), (0, 0)))
        bias_cols = jnp.stack([jnp.tile(b_sb[i], t_len) for i in range(n_b)], axis=1)
        later = [(w_out_b[i - 1], g_norm_b[i], w_in_b[i]) for i in range(1, n_b)]
        a, h = _sb_sample(seq3(q.astype(F32)), pad8(k_sh), pad8(v_sh), seq3(gate), seq3(h), bias_cols, later,
                          cache_k, cache_v, page_table)
        a, h = a.reshape(n_seq * t_len, width), h.reshape(n_seq * t_len, -1)
        w_o = w_out_b[n_b - 1]
    y = _final_stage(h, a, w_o, g_final, tm)
    return y, states, k_sh, v_sh


def kernel(x_prompt, x_sample, state_gla, cache_k, cache_v, page_table, g_norm_a, w_in_a, w_alpha2, b_alpha,
           g_head_a, w_out_a, g_kv, w_kv, g_norm_b, w_in_b, b_sb, w_out_b, g_final):
    n_p, t_p, d = x_prompt.shape
    n_s, t_s, _ = x_sample.shape
    n_main = w_in_a.shape[2] - w_alpha2.shape[1]
    width = w_kv.shape[1] // 2
    sb_shape = (SB_HEADS, width // SB_HEADS)
    row = lambda g: g[..., None, :]
    wts = (row(g_norm_a), w_in_a[:, :, :n_main].astype(BF16), w_in_a[:, :, n_main:].astype(BF16),
           w_alpha2.astype(BF16), row(b_alpha), row(g_head_a), w_out_a.astype(BF16), row(g_kv),
           w_kv.astype(BF16), w_kv[:, :width].T.astype(BF16), w_kv[:, width:].T.astype(BF16),
           row(g_norm_b), w_in_b.astype(BF16), b_sb, w_out_b.astype(BF16), row(g_final))
    y_p, s_p, k_p, v_p = _trunk(x_prompt.reshape(n_p * t_p, d), None, None, wts, n_p, t_p, 512)
    n_phys, page = cache_k.shape[:2]
    to_t = lambda c: jnp.transpose(c, (0, 2, 3, 1)).reshape(n_phys, -1, page)
    past = (to_t(cache_k), to_t(cache_v), page_table)
    y_s, s_s, k_s, v_s = _trunk(x_sample.reshape(n_s * t_s, d), state_gla, past, wts, n_s, t_s, 256)
    from_t = lambda u: jnp.transpose(u.reshape(n_p, *sb_shape, t_p), (0, 3, 1, 2))
    return (y_p.reshape(n_p, t_p, d), y_s.reshape(n_s, t_s, d), s_p, s_s, from_t(k_p), from_t(v_p),
            k_s.reshape(n_s, t_s, *sb_shape), v_s.reshape(n_s, t_s, *sb_shape))
```

```python
import functools

import jax
import jax.numpy as jnp
from jax import lax
from jax.experimental import pallas as pl
from jax.experimental.pallas import tpu as pltpu

F32 = jnp.float32
BF16 = jnp.bfloat16

RMS_EPS = 1e-6
GLA_HEADS = 4
GLA_TAU = 16.0
GLA_CHUNK = 64
SB_HEADS = 16
LANES = 128
SAMPLE_CHUNK = 16
SB_KEY_BLOCK = 256
VMEM_LIMIT = 56 * 1024 * 1024

_NT = (((1,), (1,)), ((), ()))
_TN = (((0,), (0,)), ((), ()))


def _dot(a, b):
    return jnp.dot(a, b, preferred_element_type=F32)


def _dot_nt(a, b):
    return lax.dot_general(a, b, _NT, preferred_element_type=F32)


def _dot_tn(a, b):
    return lax.dot_general(a, b, _TN, preferred_element_type=F32)


def _split_bf16(x):
    hi = x.astype(BF16)
    lo = (x - hi.astype(F32)).astype(BF16)
    return hi, lo


def _xhat(h):
    return h * lax.rsqrt(jnp.mean(h * h, axis=-1, keepdims=True) + RMS_EPS)


def _silu(x):
    return x / (1.0 + jnp.exp(-x))


LOG2_E = 1.4426950408889634
MASKED_LOGIT = -1e30


def _softplus(z):
    return jnp.maximum(z, 0.0) + jnp.log(1.0 + jnp.exp2(jnp.abs(z) * (-LOG2_E)))


def _params(*sem):
    return pltpu.CompilerParams(dimension_semantics=sem, vmem_limit_bytes=VMEM_LIMIT)


def _row_spec(tm, n):
    return pl.BlockSpec((tm, n), lambda i: (i, 0))


def _full_spec(shape):
    return pl.BlockSpec(shape, lambda i: (0,) * len(shape))


def _resid(refs, has_resid):
    if has_resid:
        h_ref, a_ref, wo_ref = refs[:3]
        return h_ref[...] + _dot(a_ref[...].astype(BF16), wo_ref[...]), refs[3:]
    return refs[0][...], refs[1:]


def _gla_in_body(has_resid, dk, dv, *refs):
    h, refs = _resid(refs, has_resid)
    g_ref, w_ref, wlr_ref, wa2_ref, ba_ref = refs[:5]
    outs = refs[5:]
    if has_resid:
        outs[0][...] = h
        outs = outs[1:]
    qk_ref, v_ref, gate_ref, la_ref = outs
    xn = (_xhat(h) * g_ref[...]).astype(BF16)
    qk_ref[...] = _dot(xn, w_ref[:, 0:2 * dk])
    v_ref[...] = _dot(xn, w_ref[:, 2 * dk:2 * dk + dv]).astype(BF16)
    gate_ref[...] = _dot(xn, w_ref[:, 2 * dk + dv:2 * dk + 2 * dv])
    lr = _dot(xn, wlr_ref[...])
    zz = _dot(lr.astype(BF16), wa2_ref[...]) + ba_ref[...]
    la_ref[...] = (-_softplus(-zz)) * (1.0 / GLA_TAU)


def _gla_in_stage(h, a, w_o, g, w_main, w_lr, w_a2, b_a, tm):
    m, d = h.shape
    dk = w_a2.shape[1]
    dv = (w_main.shape[1] - 2 * dk) // 2
    has_resid = a is not None
    ins, specs = [h], [_row_spec(tm, d)]
    if has_resid:
        ins += [a, w_o]
        specs += [_row_spec(tm, a.shape[1]), _full_spec(w_o.shape)]
    ins += [g, w_main, w_lr, w_a2, b_a]
    specs += [_full_spec(x.shape) for x in (g, w_main, w_lr, w_a2, b_a)]
    out_shape, out_specs = [], []
    if has_resid:
        out_shape.append(jax.ShapeDtypeStruct((m, d), F32))
        out_specs.append(_row_spec(tm, d))
    out_shape += [jax.ShapeDtypeStruct((m, 2 * dk), F32), jax.ShapeDtypeStruct((m, dv), BF16),
                  jax.ShapeDtypeStruct((m, dv), F32), jax.ShapeDtypeStruct((m, dk), F32)]
    out_specs += [_row_spec(tm, 2 * dk), _row_spec(tm, dv), _row_spec(tm, dv), _row_spec(tm, dk)]
    res = pl.pallas_call(
        functools.partial(_gla_in_body, has_resid, dk, dv),
        grid=(m // tm,), in_specs=specs, out_specs=out_specs, out_shape=out_shape,
        compiler_params=_params("parallel"), name="gla_in_stage")(*ins)
    if not has_resid:
        res = [h] + list(res)
    return res


def _sb_in_body(kv_mode, width, q_scale, *refs):
    h, refs = _resid(refs, True)
    if kv_mode == "rows":
        gkv_ref, wkv_ref, gb_ref, wb_ref = refs[:4]
        h_out, k_ref, v_ref, kb_ref, vb_ref, q_ref, gate_ref = refs[4:]
    elif kv_mode == "transposed":
        gkv_ref, wkt_ref, wvt_ref, gb_ref, wb_ref = refs[:5]
        h_out, kt_ref, vt_ref, ktb_ref, vtb_ref, q_ref, gate_ref = refs[5:]
    else:
        gb_ref, wb_ref = refs[:2]
        h_out, q_ref, gate_ref = refs[2:]
    h_out[...] = h
    xh = _xhat(h)
    if kv_mode is not None:
        xkv = (xh * gkv_ref[...]).astype(BF16)
        if kv_mode == "rows":
            k = _dot(xkv, wkv_ref[:, 0:width])
            v = _dot(xkv, wkv_ref[:, width:2 * width])
            k_ref[...] = k
            v_ref[...] = v
            kb_ref[...] = k.astype(BF16)
            vb_ref[...] = v.astype(BF16)
        else:
            kt = _dot_nt(wkt_ref[...], xkv)
            vt = _dot_nt(wvt_ref[...], xkv)
            kt_ref[0] = kt
            vt_ref[0] = vt
            ktb_ref[0, 0] = kt.astype(BF16)
            vtb_ref[0, 0] = vt.astype(BF16)
    xb = (xh * gb_ref[...]).astype(BF16)
    q_ref[...] = (_dot(xb, wb_ref[:, 0:width]) * q_scale).astype(BF16)
    gate_ref[...] = _dot(xb, wb_ref[:, width:2 * width])


def _sb_in_stage(h, a, w_o, kv, g_b, w_b, tm, t_len=None):
    m, d = h.shape
    width = w_b.shape[1] // 2
    kv_mode = None if kv is None else ("rows" if len(kv) == 2 else "transposed")
    ins = [h, a, w_o]
    specs = [_row_spec(tm, d), _row_spec(tm, a.shape[1]), _full_spec(w_o.shape)]
    if kv is not None:
        ins += list(kv)
        specs += [_full_spec(x.shape) for x in kv]
    ins += [g_b, w_b]
    specs += [_full_spec(g_b.shape), _full_spec(w_b.shape)]
    out_shape = [jax.ShapeDtypeStruct((m, d), F32)]
    out_specs = [_row_spec(tm, d)]
    if kv_mode == "rows":
        out_shape += [jax.ShapeDtypeStruct((m, width), F32)] * 2 + [jax.ShapeDtypeStruct((m, width), BF16)] * 2
        out_specs += [_row_spec(tm, width)] * 4
    elif kv_mode == "transposed":
        nkb = t_len // tm
        n_seq = m // t_len
        t_spec = pl.BlockSpec((1, width, tm), lambda i: (i // nkb, 0, i % nkb))
        out_shape += [jax.ShapeDtypeStruct((n_seq, width, t_len), F32)] * 2
        out_shape += [jax.ShapeDtypeStruct((n_seq, nkb, width, tm), BF16)] * 2
        out_specs += [t_spec, t_spec] + [pl.BlockSpec((1, 1, width, tm), lambda i: (i // nkb, i % nkb, 0, 0))] * 2
    out_shape += [jax.ShapeDtypeStruct((m, width), BF16), jax.ShapeDtypeStruct((m, width), F32)]
    out_specs += [_row_spec(tm, width)] * 2
    head_dim = width // SB_HEADS
    return pl.pallas_call(
        functools.partial(_sb_in_body, kv_mode, width, head_dim ** -0.5),
        grid=(m // tm,), in_specs=specs, out_specs=out_specs, out_shape=out_shape,
        compiler_params=_params("parallel"), name="sb_in_stage")(*ins)


def _final_body(h_ref, a_ref, wo_ref, g_ref, y_ref):
    h = h_ref[...] + _dot(a_ref[...].astype(BF16), wo_ref[...])
    y_ref[...] = _xhat(h) * g_ref[...]


def _final_stage(h, a, w_o, g, tm):
    m, d = h.shape
    return pl.pallas_call(
        _final_body, grid=(m // tm,),
        in_specs=[_row_spec(tm, d), _row_spec(tm, a.shape[1]), _full_spec(w_o.shape), _full_spec(g.shape)],
        out_specs=_row_spec(tm, d), out_shape=jax.ShapeDtypeStruct((m, d), F32),
        compiler_params=_params("parallel"), name="final_stage")(h, a, w_o, g)


def _gla_body(has_s0, n_prev, ns, chunk, n_chunks, q_scale, *refs):
    if has_s0:
        s0_ref, refs = refs[0], refs[1:]
    if n_prev:
        sprev_ref, refs = refs[0], refs[1:]
    qk_ref, v_ref, la_ref, gate_ref, gh_ref, og_ref, sout_ref, st_ref = refs
    c = pl.program_id(1)
    dk = la_ref.shape[-1]
    dkh = dk // GLA_HEADS
    dvh = v_ref.shape[-1] // GLA_HEADS

    @pl.when(c == 0)
    def _():
        for s in range(ns):
            for h in range(GLA_HEADS):
                st_ref[s, h] = s0_ref[0, s, h].T if has_s0 else jnp.zeros((dvh, dkh), F32)

    row = lax.broadcasted_iota(jnp.int32, (chunk, chunk), 0)
    col = lax.broadcasted_iota(jnp.int32, (chunk, chunk), 1)
    causal = row >= col
    tri = jnp.where(causal, 1.0, 0.0).astype(BF16)
    for s in range(ns):
        for n in range(n_chunks):
            sl = pl.ds(n * chunk, chunk)
            hi, lo = _split_bf16(la_ref[0, s, sl, :])
            b = _dot(tri, hi) + _dot(tri, lo)
            b_last = b[chunk - 1:chunk, :]
            q_t = (qk_ref[0, s, sl, :dk] * q_scale * jnp.exp(b)).astype(BF16)
            kk = qk_ref[0, s, sl, dk:]
            k_t = (kk * jnp.exp(-b)).astype(BF16)
            k_e = (kk * jnp.exp(b_last - b)).astype(BF16)
            decay = jnp.exp(b_last)
            for h in range(GLA_HEADS):
                kh = slice(h * dkh, (h + 1) * dkh)
                vh = slice(h * dvh, (h + 1) * dvh)
                vv = v_ref[0, s, sl, vh]
                st = st_ref[s, h]
                st_ref[s, h] = decay[:, kh] * st + _dot_tn(vv, k_e[:, kh])
                p = jnp.where(causal, _dot_nt(q_t[:, kh], k_t[:, kh]), 0.0).astype(BF16)
                o = _dot(p, vv) + _dot_nt(q_t[:, kh], st.astype(BF16))
                on = _xhat(o) * gh_ref[...]
                og_ref[0, s, sl, vh] = (on * _silu(gate_ref[0, s, sl, vh])).astype(BF16)

    @pl.when(c == pl.num_programs(1) - 1)
    def _():
        for s in range(ns):
            for h in range(GLA_HEADS):
                sout_ref[n_prev, s, h] = st_ref[s, h].T
        if n_prev:
            sout_ref[0:n_prev] = sprev_ref[...]


def _gla(qk, v, la, gate, g_head, s0, layer, s_prev, n_seq, t_len, chunk, tc, ns):
    dk = la.shape[1]
    dv = v.shape[1]
    dkh, dvh = dk // GLA_HEADS, dv // GLA_HEADS
    nct = t_len // tc
    has_s0 = s0 is not None
    seqs = lambda u: u.reshape(n_seq // ns, ns, t_len, u.shape[-1])
    tok = lambda width: pl.BlockSpec((1, ns, tc, width), lambda b, c: (b, 0, c, 0))
    n_prev = 0 if s_prev is None else s_prev.shape[0]
    state_spec = lambda n: pl.BlockSpec((n, ns, GLA_HEADS, dkh, dvh), lambda b, c: (0, b, 0, 0, 0))
    ins, specs = [], []
    if has_s0:
        ins.append(s0)
        specs.append(pl.BlockSpec((1, ns, GLA_HEADS, dkh, dvh), lambda b, c: (layer, b, 0, 0, 0)))
    if n_prev:
        ins.append(s_prev)
        specs.append(state_spec(n_prev))
    ins += [seqs(qk), seqs(v), seqs(la), seqs(gate), g_head]
    specs += [tok(2 * dk), tok(dv), tok(dk), tok(dv), pl.BlockSpec((1, dvh), lambda b, c: (0, 0))]
    og, states = pl.pallas_call(
        functools.partial(_gla_body, has_s0, n_prev, ns, chunk, tc // chunk, dkh ** -0.5),
        grid=(n_seq // ns, nct), in_specs=specs,
        out_specs=[tok(dv), state_spec(n_prev + 1)],
        out_shape=[jax.ShapeDtypeStruct((n_seq // ns, ns, t_len, dv), BF16),
                   jax.ShapeDtypeStruct((n_prev + 1, n_seq, GLA_HEADS, dkh, dvh), F32)],
        scratch_shapes=[pltpu.VMEM((ns, GLA_HEADS, dvh, dkh), F32)],
        compiler_params=_params("parallel", "arbitrary"), name="gla")(*ins)
    return og.reshape(n_seq * t_len, dv), states


def _upper_incl(n):
    j = lax.broadcasted_iota(jnp.int32, (n, n), 0)
    s = lax.broadcasted_iota(jnp.int32, (n, n), 1)
    return jnp.where(j >= s, 1.0, 0.0).astype(BF16)


def _sb_prompt_body(tq, tk, bias_ref, q_ref, kt_ref, vt_ref, gate_ref, out_ref, u_ref, *scratch):
    qh_ref, o_ref, c_ref, mb_ref = scratch[0:2], scratch[2:4], scratch[4:6], scratch[6:8]
    z_ref = (scratch[8:10], scratch[10:12])
    w_ref = (scratch[12:14], scratch[14:16])
    pair = pl.program_id(1)
    i = pl.program_id(2)
    head_dim = LANES // 2
    lane = lax.broadcasted_iota(jnp.int32, (1, LANES), 1)
    q = q_ref[...]
    zero = jnp.zeros_like(q)
    qh_ref[0][...] = jnp.where(lane < head_dim, q, zero)
    qh_ref[1][...] = jnp.where(lane >= head_dim, q, zero)
    u_ref[...] = _upper_incl(tk)
    r = lax.broadcasted_iota(jnp.int32, (tk, tk), 0)
    s = lax.broadcasted_iota(jnp.int32, (tk, tk), 1)
    bias = [bias_ref[2 * pair + hh] for hh in range(2)]
    for hh in range(2):
        o_ref[hh][...] = jnp.zeros_like(o_ref[hh])
        c_ref[hh][...] = jnp.zeros_like(c_ref[hh])
        mb_ref[hh][...] = jnp.where(s < r, bias[hh], MASKED_LOGIT)
    assert tq == 2 * tk

    def logits(j, slot, r0=0, tri0=None):
        kt = kt_ref[0, j]
        for hh in range(2):
            d = _dot(qh_ref[hh][r0:, :], kt)
            if tri0 is None:
                z_ref[slot][hh][r0:, :] = d + bias[hh]
            else:
                z_ref[slot][hh][tri0:tri0 + tk, :] = d[tri0 - r0:tri0 - r0 + tk, :] + mb_ref[hh][...]
                if tri0 + tk < tq:
                    z_ref[slot][hh][tri0 + tk:, :] = d[tri0 - r0 + tk:, :] + bias[hh]

    def suffix_sums(slot, r0=0):
        return [_dot(_softplus(z_ref[slot][hh][r0:, :]).astype(BF16), u_ref[...]) for hh in range(2)]

    def weights(slot, incls, r0=0):
        for hh in range(2):
            w = jnp.exp(z_ref[slot][hh][r0:, :] - incls[hh] - c_ref[hh][r0:, :])
            w_ref[slot][hh][r0:, :] = w.astype(BF16)
            c_ref[hh][r0:, :] += incls[hh][:, 0:1]

    def pv(j, slot, r0=0):
        vt = vt_ref[0, j]
        for hh in range(2):
            o_ref[hh][r0:, :] += _dot_nt(w_ref[slot][hh][r0:, :], vt)

    top = 2 * i + 1
    logits(top, 0, tk, tk)
    logits(top - 1, 1, 0, 0)
    weights(0, suffix_sums(0, tk), tk)
    logits(jnp.maximum(top - 2, 0), 0)
    pv(top, 0, tk)
    weights(1, suffix_sums(1))

    def body(p, carry):
        j0 = top - 2 * p
        logits(j0 - 1, 1)
        pv(j0 + 1, 1)
        weights(0, suffix_sums(0))
        logits(jnp.maximum(j0 - 2, 0), 0)
        pv(j0, 0)
        weights(1, suffix_sums(1))
        return carry

    lax.fori_loop(1, i + 1, body, 0)
    pv(0, 1)
    o = jnp.where(lane < head_dim, o_ref[0][...], o_ref[1][...])
    out_ref[...] = (o * _silu(gate_ref[...])).astype(BF16)


def _sb_prompt(q, ktb, vtb, gate, bias, n_seq, t_len, tq, tk):
    m, width = q.shape
    nq = t_len // tq
    nkb = t_len // tk
    tok = lambda b, p, i, bias_ref: (b * nq + i, p)
    blocks = pl.BlockSpec((1, nkb, LANES, tk), lambda b, p, i, bias_ref: (b, 0, p, 0))
    grid_spec = pltpu.PrefetchScalarGridSpec(
        num_scalar_prefetch=1, grid=(n_seq, width // LANES, nq),
        in_specs=[pl.BlockSpec((tq, LANES), tok), blocks, blocks, pl.BlockSpec((tq, LANES), tok)],
        out_specs=pl.BlockSpec((tq, LANES), tok),
        scratch_shapes=[pltpu.VMEM((tk, tk), BF16)] + [pltpu.VMEM((tq, LANES), BF16)] * 2
        + [pltpu.VMEM((tq, LANES), F32)] * 2 + [pltpu.VMEM((tq, 1), F32)] * 2 + [pltpu.VMEM((tk, tk), F32)] * 2
        + [pltpu.VMEM((tq, tk), F32)] * 4 + [pltpu.VMEM((tq, tk), BF16)] * 4)
    return pl.pallas_call(
        functools.partial(_sb_prompt_body, tq, tk), grid_spec=grid_spec,
        out_shape=jax.ShapeDtypeStruct((m, width), BF16),
        compiler_params=_params("parallel", "parallel", "arbitrary"), name="sb_prompt")(bias, q, ktb, vtb, gate)


def _sb_sample_attend(q, gate, bias, kn, vn, kt_pages, vt_pages, page):
    t_new, width = q.shape
    n_pages = len(kt_pages)
    head_dim = width // SB_HEADS
    rows = t_new * SB_HEADS
    head_of_lane = lax.broadcasted_iota(jnp.int32, (SB_HEADS, width), 1) // head_dim
    head_of_row = lax.broadcasted_iota(jnp.int32, (SB_HEADS, width), 0)
    own = head_of_lane == head_of_row
    q_bd = jnp.concatenate(
        [jnp.where(own, jnp.broadcast_to(q[t:t + 1, :], (SB_HEADS, width)), 0.0) for t in range(t_new)],
        axis=0).astype(BF16)
    u_incl = _upper_incl(2 * page)
    t_of_row = lax.broadcasted_iota(jnp.int32, (rows, page), 0) // SB_HEADS
    c_idx = lax.broadcasted_iota(jnp.int32, (rows, page), 1)
    z_new = _dot_nt(q_bd, kn) + jnp.where(c_idx < t_of_row, bias, MASKED_LOGIT)
    z_pages = [_dot(q_bd, kt_pages[p][0].astype(BF16)) + bias for p in range(n_pages)]
    z_blocks = [jnp.concatenate([z_pages[2 * j], z_pages[2 * j + 1]], axis=1) for j in range(n_pages // 2)]
    sp_new = _softplus(z_new)
    sp_blocks = [_softplus(z) for z in z_blocks]
    incl_new = _dot(sp_new.astype(BF16), u_incl[:page, :page])
    incl_blocks = [_dot(sp.astype(BF16), u_incl) for sp in sp_blocks]
    carry = jnp.sum(sp_new, axis=1, keepdims=True)
    carries = [None] * len(z_blocks)
    for j in reversed(range(len(z_blocks))):
        carries[j] = carry
        carry = carry + jnp.sum(sp_blocks[j], axis=1, keepdims=True)
    w_new = jnp.exp(z_new - incl_new).astype(BF16)
    w_blocks = [jnp.exp(z - incl - c).astype(BF16) for z, incl, c in zip(z_blocks, incl_blocks, carries)]
    o = _dot(w_new, vn)
    for j in range(len(z_blocks)):
        o = (o + _dot_nt(w_blocks[j][:, :page], vt_pages[2 * j][0].astype(BF16))
             + _dot_nt(w_blocks[j][:, page:], vt_pages[2 * j + 1][0].astype(BF16)))
    out = jnp.concatenate(
        [jnp.sum(jnp.where(own, o[t * SB_HEADS:(t + 1) * SB_HEADS, :], 0.0), axis=0, keepdims=True)
         for t in range(t_new)], axis=0)
    return out * _silu(gate)


def _sb_sample_body(n_layers, n_pages, page, q_scale, pt_ref, bias_ref, q_ref, knew_ref, vnew_ref, gate_ref,
                    h_ref, *refs):
    layer_refs = refs[:3 * (n_layers - 1)]
    refs = refs[3 * (n_layers - 1):]
    kt_pages = refs[:n_pages]
    vt_pages = refs[n_pages:2 * n_pages]
    a_ref, hout_ref = refs[2 * n_pages:]
    width = q_ref.shape[-1]
    pad = jnp.zeros((page - knew_ref.shape[1], width), F32)
    kn = jnp.concatenate([knew_ref[0], pad], axis=0).astype(BF16)
    vn = jnp.concatenate([vnew_ref[0], pad], axis=0).astype(BF16)
    q, gate, h = q_ref[0], gate_ref[0], h_ref[0]
    for layer in range(n_layers):
        if layer > 0:
            wo_ref, g_ref, win_ref = layer_refs[3 * (layer - 1):3 * layer]
            h = h + _dot(a.astype(BF16), wo_ref[...])
            xn = (_xhat(h) * g_ref[...]).astype(BF16)
            q = (_dot(xn, win_ref[:, 0:width]) * q_scale).astype(BF16).astype(F32)
            gate = _dot(xn, win_ref[:, width:2 * width])
        a = _sb_sample_attend(q, gate, bias_ref[:, layer:layer + 1], kn, vn, kt_pages, vt_pages, page)
    a_ref[0] = a
    hout_ref[0] = h


def _sb_sample(q, k_new, v_new, gate, h, bias_cols, later_layers, cache_kt, cache_vt, page_table):
    n_seq, t_new, width = q.shape
    n_pages = page_table.shape[1]
    page = cache_kt.shape[2]
    n_layers = 1 + len(later_layers)
    pt_flat = page_table.reshape(-1)
    tok = lambda b, pt: (b, 0, 0)
    const = lambda x: pl.BlockSpec(x.shape, lambda b, pt: (0,) * x.ndim)

    def page_spec(j):
        return pl.BlockSpec((1, width, page), lambda b, pt: (pt[b * n_pages + j], 0, 0))

    weights = [x for layer in later_layers for x in layer]
    grid_spec = pltpu.PrefetchScalarGridSpec(
        num_scalar_prefetch=1, grid=(n_seq,),
        in_specs=[const(bias_cols),
                  pl.BlockSpec((1, t_new, width), tok), pl.BlockSpec((1,) + k_new.shape[1:], tok),
                  pl.BlockSpec((1,) + v_new.shape[1:], tok), pl.BlockSpec((1, t_new, width), tok),
                  pl.BlockSpec((1,) + h.shape[1:], tok)]
                 + [const(x) for x in weights] + [page_spec(j) for j in range(n_pages)] * 2,
        out_specs=[pl.BlockSpec((1, t_new, width), tok), pl.BlockSpec((1,) + h.shape[1:], tok)])
    head_dim = width // SB_HEADS
    return pl.pallas_call(
        functools.partial(_sb_sample_body, n_layers, n_pages, page, head_dim ** -0.5), grid_spec=grid_spec,
        out_shape=[jax.ShapeDtypeStruct((n_seq, t_new, width), F32), jax.ShapeDtypeStruct(h.shape, F32)],
        compiler_params=_params("parallel"), name="sb_sample")(
            pt_flat, bias_cols, q, k_new, v_new, gate, h, *weights,
            *([cache_kt] * n_pages), *([cache_vt] * n_pages))


def _trunk(x, s_init, past, wts, n_seq, t_len, tm):
    (g_norm_a, w_main_a, w_lr_a, w_a2, b_a, g_head_a, w_out_a, g_kv, w_kv, w_kt, w_vt, g_norm_b, w_in_b, b_sb,
     w_out_b, g_final) = wts
    n_a = w_main_a.shape[0]
    n_b = w_in_b.shape[0]
    is_prompt = past is None
    h, a, w_o = x, None, None
    states = None
    for i in range(n_a):
        h, qk, v, gate, la = _gla_in_stage(h, a, w_o, g_norm_a[i], w_main_a[i], w_lr_a[i], w_a2[i], b_a[i], tm)
        if is_prompt:
            a, states = _gla(qk, v, la, gate, g_head_a[i], None, i, states, n_seq, t_len, GLA_CHUNK,
                             4 * GLA_CHUNK, 1)
        else:
            def padt(u):
                u = u.reshape(n_seq, t_len, u.shape[-1])
                u = jnp.pad(u, ((0, 0), (0, SAMPLE_CHUNK - t_len), (0, 0)))
                return u.reshape(n_seq * SAMPLE_CHUNK, u.shape[-1])
            a, states = _gla(padt(qk), padt(v), padt(la), padt(gate), g_head_a[i], s_init, i, states, n_seq,
                             SAMPLE_CHUNK, SAMPLE_CHUNK, SAMPLE_CHUNK, 4)
            a = a.reshape(n_seq, SAMPLE_CHUNK, -1)[:, :t_len].reshape(n_seq * t_len, -1)
        w_o = w_out_a[i]
    if is_prompt:
        for i in range(n_b):
            if i == 0:
                h, k_sh, v_sh, ktb, vtb, q, gate = _sb_in_stage(h, a, w_o, (g_kv, w_kt, w_vt), g_norm_b[i],
                                                                w_in_b[i], SB_KEY_BLOCK, t_len)
            else:
                h, q, gate = _sb_in_stage(h, a, w_o, None, g_norm_b[i], w_in_b[i], tm)
            a = _sb_prompt(q, ktb, vtb, gate, b_sb[i], n_seq, t_len, 2 * SB_KEY_BLOCK, SB_KEY_BLOCK)
            w_o = w_out_b[i]
    else:
        cache_k, cache_v, page_table = past
        h, k_sh, v_sh, _, _, q, gate = _sb_in_stage(h, a, w_o, (g_kv, w_kv), g_norm_b[0], w_in_b[0], tm)
        width = q.shape[1]
        seq3 = lambda u: u.reshape(n_seq, t_len, u.shape[-1])
        pad8 = lambda u: jnp.pad(seq3(u), ((0, 0), (0, 8 - t_len), (0, 0)))
        bias_cols = jnp.stack([jnp.tile(b_sb[i], t_len) for i in range(n_b)], axis=1)
        later = [(w_out_b[i - 1], g_norm_b[i], w_in_b[i]) for i in range(1, n_b)]
        a, h = _sb_sample(seq3(q.astype(F32)), pad8(k_sh), pad8(v_sh), seq3(gate), seq3(h), bias_cols, later,
                          cache_k, cache_v, page_table)
        a, h = a.reshape(n_seq * t_len, width), h.reshape(n_seq * t_len, -1)
        w_o = w_out_b[n_b - 1]
    y = _final_stage(h, a, w_o, g_final, tm)
    return y, states, k_sh, v_sh


def kernel(x_prompt, x_sample, state_gla, cache_k, cache_v, page_table, g_norm_a, w_in_a, w_alpha2, b_alpha,
           g_head_a, w_out_a, g_kv, w_kv, g_norm_b, w_in_b, b_sb, w_out_b, g_final):
    n_p, t_p, d = x_prompt.shape
    n_s, t_s, _ = x_sample.shape
    n_main = w_in_a.shape[2] - w_alpha2.shape[1]
    width = w_kv.shape[1] // 2
    sb_shape = (SB_HEADS, width // SB_HEADS)
    row = lambda g: g[..., None, :]
    wts = (row(g_norm_a), w_in_a[:, :, :n_main].astype(BF16), w_in_a[:, :, n_main:].astype(BF16),
           w_alpha2.astype(BF16), row(b_alpha), row(g_head_a), w_out_a.astype(BF16), row(g_kv),
           w_kv.astype(BF16), w_kv[:, :width].T.astype(BF16), w_kv[:, width:].T.astype(BF16),
           row(g_norm_b), w_in_b.astype(BF16), b_sb, w_out_b.astype(BF16), row(g_final))
    y_p, s_p, k_p, v_p = _trunk(x_prompt.reshape(n_p * t_p, d), None, None, wts, n_p, t_p, 512)
    n_phys, page = cache_k.shape[:2]
    to_t = lambda c: jnp.transpose(c, (0, 2, 3, 1)).reshape(n_phys, -1, page)
    past = (to_t(cache_k), to_t(cache_v), page_table)
    y_s, s_s, k_s, v_s = _trunk(x_sample.reshape(n_s * t_s, d), state_gla, past, wts, n_s, t_s, 256)
    from_t = lambda u: jnp.transpose(u.reshape(n_p, *sb_shape, t_p), (0, 3, 1, 2))
    return (y_p.reshape(n_p, t_p, d), y_s.reshape(n_s, t_s, d), s_p, s_s, from_t(k_p), from_t(v_p),
            k_s.reshape(n_s, t_s, *sb_shape), v_s.reshape(n_s, t_s, *sb_shape))
```

```python
import functools

import jax
import jax.numpy as jnp
from jax import lax
from jax.experimental import pallas as pl
from jax.experimental.pallas import tpu as pltpu

F32 = jnp.float32
BF16 = jnp.bfloat16

RMS_EPS = 1e-6
GLA_HEADS = 4
GLA_TAU = 16.0
GLA_CHUNK = 64
SB_HEADS = 16
LANES = 128
SAMPLE_CHUNK = 16
SB_KEY_BLOCK = 256
VMEM_LIMIT = 56 * 1024 * 1024

_NT = (((1,), (1,)), ((), ()))
_TN = (((0,), (0,)), ((), ()))


def _dot(a, b):
    return jnp.dot(a, b, preferred_element_type=F32)


def _dot_nt(a, b):
    return lax.dot_general(a, b, _NT, preferred_element_type=F32)


def _dot_tn(a, b):
    return lax.dot_general(a, b, _TN, preferred_element_type=F32)


def _split_bf16(x):
    hi = x.astype(BF16)
    lo = (x - hi.astype(F32)).astype(BF16)
    return hi, lo


def _xhat(h):
    return h * lax.rsqrt(jnp.mean(h * h, axis=-1, keepdims=True) + RMS_EPS)


def _silu(x):
    return x / (1.0 + jnp.exp(-x))


LOG2_E = 1.4426950408889634
MASKED_LOGIT = -1e30


def _softplus(z):
    return jnp.maximum(z, 0.0) + jnp.log(1.0 + jnp.exp2(jnp.abs(z) * (-LOG2_E)))


def _params(*sem):
    return pltpu.CompilerParams(dimension_semantics=sem, vmem_limit_bytes=VMEM_LIMIT)


def _row_spec(tm, n):
    return pl.BlockSpec((tm, n), lambda i: (i, 0))


def _full_spec(shape):
    return pl.BlockSpec(shape, lambda i: (0,) * len(shape))


def _resid(refs, has_resid):
    if has_resid:
        h_ref, a_ref, wo_ref = refs[:3]
        return h_ref[...] + _dot(a_ref[...].astype(BF16), wo_ref[...]), refs[3:]
    return refs[0][...], refs[1:]


def _gla_in_body(has_resid, dk, dv, *refs):
    h, refs = _resid(refs, has_resid)
    g_ref, w_ref, wlr_ref, wa2_ref, ba_ref = refs[:5]
    outs = refs[5:]
    if has_resid:
        outs[0][...] = h
        outs = outs[1:]
    qk_ref, v_ref, gate_ref, la_ref = outs
    xn = (_xhat(h) * g_ref[...]).astype(BF16)
    qk_ref[...] = _dot(xn, w_ref[:, 0:2 * dk])
    v_ref[...] = _dot(xn, w_ref[:, 2 * dk:2 * dk + dv]).astype(BF16)
    gate_ref[...] = _dot(xn, w_ref[:, 2 * dk + dv:2 * dk + 2 * dv])
    lr = _dot(xn, wlr_ref[...])
    zz = _dot(lr.astype(BF16), wa2_ref[...]) + ba_ref[...]
    la_ref[...] = (-_softplus(-zz)) * (1.0 / GLA_TAU)


def _gla_in_stage(h, a, w_o, g, w_main, w_lr, w_a2, b_a, tm):
    m, d = h.shape
    dk = w_a2.shape[1]
    dv = (w_main.shape[1] - 2 * dk) // 2
    has_resid = a is not None
    ins, specs = [h], [_row_spec(tm, d)]
    if has_resid:
        ins += [a, w_o]
        specs += [_row_spec(tm, a.shape[1]), _full_spec(w_o.shape)]
    ins += [g, w_main, w_lr, w_a2, b_a]
    specs += [_full_spec(x.shape) for x in (g, w_main, w_lr, w_a2, b_a)]
    out_shape, out_specs = [], []
    if has_resid:
        out_shape.append(jax.ShapeDtypeStruct((m, d), F32))
        out_specs.append(_row_spec(tm, d))
    out_shape += [jax.ShapeDtypeStruct((m, 2 * dk), F32), jax.ShapeDtypeStruct((m, dv), BF16),
                  jax.ShapeDtypeStruct((m, dv), F32), jax.ShapeDtypeStruct((m, dk), F32)]
    out_specs += [_row_spec(tm, 2 * dk), _row_spec(tm, dv), _row_spec(tm, dv), _row_spec(tm, dk)]
    res = pl.pallas_call(
        functools.partial(_gla_in_body, has_resid, dk, dv),
        grid=(m // tm,), in_specs=specs, out_specs=out_specs, out_shape=out_shape,
        compiler_params=_params("parallel"), name="gla_in_stage")(*ins)
    if not has_resid:
        res = [h] + list(res)
    return res


def _sb_in_body(kv_mode, width, q_scale, *refs):
    h, refs = _resid(refs, True)
    if kv_mode == "rows":
        gkv_ref, wkv_ref, gb_ref, wb_ref = refs[:4]
        h_out, k_ref, v_ref, kb_ref, vb_ref, q_ref, gate_ref = refs[4:]
    elif kv_mode == "transposed":
        gkv_ref, wkt_ref, wvt_ref, gb_ref, wb_ref = refs[:5]
        h_out, kt_ref, vt_ref, ktb_ref, vtb_ref, q_ref, gate_ref = refs[5:]
    else:
        gb_ref, wb_ref = refs[:2]
        h_out, q_ref, gate_ref = refs[2:]
    h_out[...] = h
    xh = _xhat(h)
    if kv_mode is not None:
        xkv = (xh * gkv_ref[...]).astype(BF16)
        if kv_mode == "rows":
            k = _dot(xkv, wkv_ref[:, 0:width])
            v = _dot(xkv, wkv_ref[:, width:2 * width])
            k_ref[...] = k
            v_ref[...] = v
            kb_ref[...] = k.astype(BF16)
            vb_ref[...] = v.astype(BF16)
        else:
            kt = _dot_nt(wkt_ref[...], xkv)
            vt = _dot_nt(wvt_ref[...], xkv)
            kt_ref[0] = kt
            vt_ref[0] = vt
            ktb_ref[0, 0] = kt.astype(BF16)
            vtb_ref[0, 0] = vt.astype(BF16)
    xb = (xh * gb_ref[...]).astype(BF16)
    q_ref[...] = (_dot(xb, wb_ref[:, 0:width]) * q_scale).astype(BF16)
    gate_ref[...] = _dot(xb, wb_ref[:, width:2 * width])


def _sb_in_stage(h, a, w_o, kv, g_b, w_b, tm, t_len=None):
    m, d = h.shape
    width = w_b.shape[1] // 2
    kv_mode = None if kv is None else ("rows" if len(kv) == 2 else "transposed")
    ins = [h, a, w_o]
    specs = [_row_spec(tm, d), _row_spec(tm, a.shape[1]), _full_spec(w_o.shape)]
    if kv is not None:
        ins += list(kv)
        specs += [_full_spec(x.shape) for x in kv]
    ins += [g_b, w_b]
    specs += [_full_spec(g_b.shape), _full_spec(w_b.shape)]
    out_shape = [jax.ShapeDtypeStruct((m, d), F32)]
    out_specs = [_row_spec(tm, d)]
    if kv_mode == "rows":
        out_shape += [jax.ShapeDtypeStruct((m, width), F32)] * 2 + [jax.ShapeDtypeStruct((m, width), BF16)] * 2
        out_specs += [_row_spec(tm, width)] * 4
    elif kv_mode == "transposed":
        nkb = t_len // tm
        n_seq = m // t_len
        t_spec = pl.BlockSpec((1, width, tm), lambda i: (i // nkb, 0, i % nkb))
        out_shape += [jax.ShapeDtypeStruct((n_seq, width, t_len), F32)] * 2
        out_shape += [jax.ShapeDtypeStruct((n_seq, nkb, width, tm), BF16)] * 2
        out_specs += [t_spec, t_spec] + [pl.BlockSpec((1, 1, width, tm), lambda i: (i // nkb, i % nkb, 0, 0))] * 2
    out_shape += [jax.ShapeDtypeStruct((m, width), BF16), jax.ShapeDtypeStruct((m, width), F32)]
    out_specs += [_row_spec(tm, width)] * 2
    head_dim = width // SB_HEADS
    return pl.pallas_call(
        functools.partial(_sb_in_body, kv_mode, width, head_dim ** -0.5),
        grid=(m // tm,), in_specs=specs, out_specs=out_specs, out_shape=out_shape,
        compiler_params=_params("parallel"), name="sb_in_stage")(*ins)


def _final_body(h_ref, a_ref, wo_ref, g_ref, y_ref):
    h = h_ref[...] + _dot(a_ref[...].astype(BF16), wo_ref[...])
    y_ref[...] = _xhat(h) * g_ref[...]


def _final_stage(h, a, w_o, g, tm):
    m, d = h.shape
    return pl.pallas_call(
        _final_body, grid=(m // tm,),
        in_specs=[_row_spec(tm, d), _row_spec(tm, a.shape[1]), _full_spec(w_o.shape), _full_spec(g.shape)],
        out_specs=_row_spec(tm, d), out_shape=jax.ShapeDtypeStruct((m, d), F32),
        compiler_params=_params("parallel"), name="final_stage")(h, a, w_o, g)


def _gla_body(has_s0, n_prev, ns, chunk, n_chunks, q_scale, *refs):
    if has_s0:
        s0_ref, refs = refs[0], refs[1:]
    if n_prev:
        sprev_ref, refs = refs[0], refs[1:]
    qk_ref, v_ref, la_ref, gate_ref, gh_ref, og_ref, sout_ref, st_ref = refs
    c = pl.program_id(1)
    dk = la_ref.shape[-1]
    dkh = dk // GLA_HEADS
    dvh = v_ref.shape[-1] // GLA_HEADS

    @pl.when(c == 0)
    def _():
        for s in range(ns):
            for h in range(GLA_HEADS):
                st_ref[s, h] = s0_ref[0, s, h].T if has_s0 else jnp.zeros((dvh, dkh), F32)

    row = lax.broadcasted_iota(jnp.int32, (chunk, chunk), 0)
    col = lax.broadcasted_iota(jnp.int32, (chunk, chunk), 1)
    causal = row >= col
    tri = jnp.where(causal, 1.0, 0.0).astype(BF16)
    for s in range(ns):
        for n in range(n_chunks):
            sl = pl.ds(n * chunk, chunk)
            hi, lo = _split_bf16(la_ref[0, s, sl, :])
            b = _dot(tri, hi) + _dot(tri, lo)
            b_last = b[chunk - 1:chunk, :]
            q_t = (qk_ref[0, s, sl, :dk] * q_scale * jnp.exp(b)).astype(BF16)
            kk = qk_ref[0, s, sl, dk:]
            k_t = (kk * jnp.exp(-b)).astype(BF16)
            k_e = (kk * jnp.exp(b_last - b)).astype(BF16)
            decay = jnp.exp(b_last)
            for h in range(GLA_HEADS):
                kh = slice(h * dkh, (h + 1) * dkh)
                vh = slice(h * dvh, (h + 1) * dvh)
                vv = v_ref[0, s, sl, vh]
                st = st_ref[s, h]
                st_ref[s, h] = decay[:, kh] * st + _dot_tn(vv, k_e[:, kh])
                p = jnp.where(causal, _dot_nt(q_t[:, kh], k_t[:, kh]), 0.0).astype(BF16)
                o = _dot(p, vv) + _dot_nt(q_t[:, kh], st.astype(BF16))
                on = _xhat(o) * gh_ref[...]
                og_ref[0, s, sl, vh] = (on * _silu(gate_ref[0, s, sl, vh])).astype(BF16)

    @pl.when(c == pl.num_programs(1) - 1)
    def _():
        for s in range(ns):
            for h in range(GLA_HEADS):
                sout_ref[n_prev, s, h] = st_ref[s, h].T
        if n_prev:
            sout_ref[0:n_prev] = sprev_ref[...]


def _gla(qk, v, la, gate, g_head, s0, layer, s_prev, n_seq, t_len, chunk, tc, ns):
    dk = la.shape[1]
    dv = v.shape[1]
    dkh, dvh = dk // GLA_HEADS, dv // GLA_HEADS
    nct = t_len // tc
    has_s0 = s0 is not None
    seqs = lambda u: u.reshape(n_seq // ns, ns, t_len, u.shape[-1])
    tok = lambda width: pl.BlockSpec((1, ns, tc, width), lambda b, c: (b, 0, c, 0))
    n_prev = 0 if s_prev is None else s_prev.shape[0]
    state_spec = lambda n: pl.BlockSpec((n, ns, GLA_HEADS, dkh, dvh), lambda b, c: (0, b, 0, 0, 0))
    ins, specs = [], []
    if has_s0:
        ins.append(s0)
        specs.append(pl.BlockSpec((1, ns, GLA_HEADS, dkh, dvh), lambda b, c: (layer, b, 0, 0, 0)))
    if n_prev:
        ins.append(s_prev)
        specs.append(state_spec(n_prev))
    ins += [seqs(qk), seqs(v), seqs(la), seqs(gate), g_head]
    specs += [tok(2 * dk), tok(dv), tok(dk), tok(dv), pl.BlockSpec((1, dvh), lambda b, c: (0, 0))]
    og, states = pl.pallas_call(
        functools.partial(_gla_body, has_s0, n_prev, ns, chunk, tc // chunk, dkh ** -0.5),
        grid=(n_seq // ns, nct), in_specs=specs,
        out_specs=[tok(dv), state_spec(n_prev + 1)],
        out_shape=[jax.ShapeDtypeStruct((n_seq // ns, ns, t_len, dv), BF16),
                   jax.ShapeDtypeStruct((n_prev + 1, n_seq, GLA_HEADS, dkh, dvh), F32)],
        scratch_shapes=[pltpu.VMEM((ns, GLA_HEADS, dvh, dkh), F32)],
        compiler_params=_params("parallel", "arbitrary"), name="gla")(*ins)
    return og.reshape(n_seq * t_len, dv), states


def _upper_incl(n):
    j = lax.broadcasted_iota(jnp.int32, (n, n), 0)
    s = lax.broadcasted_iota(jnp.int32, (n, n), 1)
    return jnp.where(j >= s, 1.0, 0.0).astype(BF16)


def _sb_prompt_body(tq, tk, bias_ref, q_ref, kt_ref, vt_ref, gate_ref, out_ref, u_ref, *scratch):
    qh_ref, o_ref, c_ref, mb_ref = scratch[0:2], scratch[2:4], scratch[4:6], scratch[6:8]
    z_ref = (scratch[8:10], scratch[10:12])
    w_ref = (scratch[12:14], scratch[14:16])
    pair = pl.program_id(1)
    i = pl.program_id(2)
    head_dim = LANES // 2
    lane = lax.broadcasted_iota(jnp.int32, (1, LANES), 1)
    q = q_ref[...]
    zero = jnp.zeros_like(q)
    qh_ref[0][...] = jnp.where(lane < head_dim, q, zero)
    qh_ref[1][...] = jnp.where(lane >= head_dim, q, zero)
    u_ref[...] = _upper_incl(tk)
    r = lax.broadcasted_iota(jnp.int32, (tk, tk), 0)
    s = lax.broadcasted_iota(jnp.int32, (tk, tk), 1)
    bias = [bias_ref[2 * pair + hh] for hh in range(2)]
    for hh in range(2):
        o_ref[hh][...] = jnp.zeros_like(o_ref[hh])
        c_ref[hh][...] = jnp.zeros_like(c_ref[hh])
        mb_ref[hh][...] = jnp.where(s < r, bias[hh], MASKED_LOGIT)
    assert tq == 2 * tk

    def logits(j, slot, r0=0, tri0=None):
        kt = kt_ref[0, j]
        for hh in range(2):
            d = _dot(qh_ref[hh][r0:, :], kt)
            if tri0 is None:
                z_ref[slot][hh][r0:, :] = d + bias[hh]
            else:
                z_ref[slot][hh][tri0:tri0 + tk, :] = d[tri0 - r0:tri0 - r0 + tk, :] + mb_ref[hh][...]
                if tri0 + tk < tq:
                    z_ref[slot][hh][tri0 + tk:, :] = d[tri0 - r0 + tk:, :] + bias[hh]

    def suffix_sums(slot, r0=0):
        return [_dot(_softplus(z_ref[slot][hh][r0:, :]).astype(BF16), u_ref[...]) for hh in range(2)]

    def weights(slot, incls, r0=0):
        for hh in range(2):
            w = jnp.exp(z_ref[slot][hh][r0:, :] - incls[hh] - c_ref[hh][r0:, :])
            w_ref[slot][hh][r0:, :] = w.astype(BF16)
            c_ref[hh][r0:, :] += incls[hh][:, 0:1]

    def pv(j, slot, r0=0):
        vt = vt_ref[0, j]
        for hh in range(2):
            o_ref[hh][r0:, :] += _dot_nt(w_ref[slot][hh][r0:, :], vt)

    top = 2 * i + 1
    logits(top, 0, tk, tk)
    logits(top - 1, 1, 0, 0)
    weights(0, suffix_sums(0, tk), tk)
    logits(jnp.maximum(top - 2, 0), 0)
    pv(top, 0, tk)
    weights(1, suffix_sums(1))

    def body(p, carry):
        j0 = top - 2 * p
        logits(j0 - 1, 1)
        pv(j0 + 1, 1)
        weights(0, suffix_sums(0))
        logits(jnp.maximum(j0 - 2, 0), 0)
        pv(j0, 0)
        weights(1, suffix_sums(1))
        return carry

    lax.fori_loop(1, i + 1, body, 0)
    pv(0, 1)
    o = jnp.where(lane < head_dim, o_ref[0][...], o_ref[1][...])
    out_ref[...] = (o * _silu(gate_ref[...])).astype(BF16)


def _sb_prompt(q, ktb, vtb, gate, bias, n_seq, t_len, tq, tk):
    m, width = q.shape
    nq = t_len // tq
    nkb = t_len // tk
    tok = lambda b, p, i, bias_ref: (b * nq + i, p)
    blocks = pl.BlockSpec((1, nkb, LANES, tk), lambda b, p, i, bias_ref: (b, 0, p, 0))
    grid_spec = pltpu.PrefetchScalarGridSpec(
        num_scalar_prefetch=1, grid=(n_seq, width // LANES, nq),
        in_specs=[pl.BlockSpec((tq, LANES), tok), blocks, blocks, pl.BlockSpec((tq, LANES), tok)],
        out_specs=pl.BlockSpec((tq, LANES), tok),
        scratch_shapes=[pltpu.VMEM((tk, tk), BF16)] + [pltpu.VMEM((tq, LANES), BF16)] * 2
        + [pltpu.VMEM((tq, LANES), F32)] * 2 + [pltpu.VMEM((tq, 1), F32)] * 2 + [pltpu.VMEM((tk, tk), F32)] * 2
        + [pltpu.VMEM((tq, tk), F32)] * 4 + [pltpu.VMEM((tq, tk), BF16)] * 4)
    return pl.pallas_call(
        functools.partial(_sb_prompt_body, tq, tk), grid_spec=grid_spec,
        out_shape=jax.ShapeDtypeStruct((m, width), BF16),
        compiler_params=_params("parallel", "parallel", "arbitrary"), name="sb_prompt")(bias, q, ktb, vtb, gate)


def _sb_sample_attend(q, gate, bias, kn, vn, kt_pages, vt_pages, page):
    t_new, width = q.shape
    n_pages = len(kt_pages)
    head_dim = width // SB_HEADS
    rows = t_new * SB_HEADS
    head_of_lane = lax.broadcasted_iota(jnp.int32, (SB_HEADS, width), 1) // head_dim
    head_of_row = lax.broadcasted_iota(jnp.int32, (SB_HEADS, width), 0)
    own = head_of_lane == head_of_row
    q_bd = jnp.concatenate(
        [jnp.where(own, jnp.broadcast_to(q[t:t + 1, :], (SB_HEADS, width)), 0.0) for t in range(t_new)],
        axis=0).astype(BF16)
    u_incl = _upper_incl(2 * page)
    t_of_row = lax.broadcasted_iota(jnp.int32, (rows, page), 0) // SB_HEADS
    c_idx = lax.broadcasted_iota(jnp.int32, (rows, page), 1)
    z_new = _dot_nt(q_bd, kn) + jnp.where(c_idx < t_of_row, bias, MASKED_LOGIT)
    z_pages = [_dot(q_bd, kt_pages[p][0].astype(BF16)) + bias for p in range(n_pages)]
    z_blocks = [jnp.concatenate([z_pages[2 * j], z_pages[2 * j + 1]], axis=1) for j in range(n_pages // 2)]
    sp_new = _softplus(z_new)
    sp_blocks = [_softplus(z) for z in z_blocks]
    incl_new = _dot(sp_new.astype(BF16), u_incl[:page, :page])
    incl_blocks = [_dot(sp.astype(BF16), u_incl) for sp in sp_blocks]
    carry = jnp.sum(sp_new, axis=1, keepdims=True)
    carries = [None] * len(z_blocks)
    for j in reversed(range(len(z_blocks))):
        carries[j] = carry
        carry = carry + jnp.sum(sp_blocks[j], axis=1, keepdims=True)
    w_new = jnp.exp(z_new - incl_new).astype(BF16)
    w_blocks = [jnp.exp(z - incl - c).astype(BF16) for z, incl, c in zip(z_blocks, incl_blocks, carries)]
    o = _dot(w_new, vn)
    for j in range(len(z_blocks)):
        o = (o + _dot_nt(w_blocks[j][:, :page], vt_pages[2 * j][0].astype(BF16))
             + _dot_nt(w_blocks[j][:, page:], vt_pages[2 * j + 1][0].astype(BF16)))
    out = jnp.concatenate(
        [jnp.sum(jnp.where(own, o[t * SB_HEADS:(t + 1) * SB_HEADS, :], 0.0), axis=0, keepdims=True)
         for t in range(t_new)], axis=0)
    return out * _silu(gate)


def _sb_sample_body(n_layers, n_pages, page, q_scale, pt_ref, bias_ref, q_ref, knew_ref, vnew_ref, gate_ref,
                    h_ref, *refs):
    layer_refs = refs[:3 * (n_layers - 1)]
    refs = refs[3 * (n_layers - 1):]
    kt_pages = refs[:n_pages]
    vt_pages = refs[n_pages:2 * n_pages]
    a_ref, hout_ref = refs[2 * n_pages:]
    width = q_ref.shape[-1]
    pad = jnp.zeros((page - knew_ref.shape[1], width), F32)
    kn = jnp.concatenate([knew_ref[0], pad], axis=0).astype(BF16)
    vn = jnp.concatenate([vnew_ref[0], pad], axis=0).astype(BF16)
    q, gate, h = q_ref[0], gate_ref[0], h_ref[0]
    for layer in range(n_layers):
        if layer > 0:
            wo_ref, g_ref, win_ref = layer_refs[3 * (layer - 1):3 * layer]
            h = h + _dot(a.astype(BF16), wo_ref[...])
            xn = (_xhat(h) * g_ref[...]).astype(BF16)
            q = (_dot(xn, win_ref[:, 0:width]) * q_scale).astype(BF16).astype(F32)
            gate = _dot(xn, win_ref[:, width:2 * width])
        a = _sb_sample_attend(q, gate, bias_ref[:, layer:layer + 1], kn, vn, kt_pages, vt_pages, page)
    a_ref[0] = a
    hout_ref[0] = h


def _sb_sample(q, k_new, v_new, gate, h, bias_cols, later_layers, cache_kt, cache_vt, page_table):
    n_seq, t_new, width = q.shape
    n_pages = page_table.shape[1]
    page = cache_kt.shape[2]
    n_layers = 1 + len(later_layers)
    pt_flat = page_table.reshape(-1)
    tok = lambda b, pt: (b, 0, 0)
    const = lambda x: pl.BlockSpec(x.shape, lambda b, pt: (0,) * x.ndim)

    def page_spec(j):
        return pl.BlockSpec((1, width, page), lambda b, pt: (pt[b * n_pages + j], 0, 0))

    weights = [x for layer in later_layers for x in layer]
    grid_spec = pltpu.PrefetchScalarGridSpec(
        num_scalar_prefetch=1, grid=(n_seq,),
        in_specs=[const(bias_cols),
                  pl.BlockSpec((1, t_new, width), tok), pl.BlockSpec((1,) + k_new.shape[1:], tok),
                  pl.BlockSpec((1,) + v_new.shape[1:], tok), pl.BlockSpec((1, t_new, width), tok),
                  pl.BlockSpec((1,) + h.shape[1:], tok)]
                 + [const(x) for x in weights] + [page_spec(j) for j in range(n_pages)] * 2,
        out_specs=[pl.BlockSpec((1, t_new, width), tok), pl.BlockSpec((1,) + h.shape[1:], tok)])
    head_dim = width // SB_HEADS
    return pl.pallas_call(
        functools.partial(_sb_sample_body, n_layers, n_pages, page, head_dim ** -0.5), grid_spec=grid_spec,
        out_shape=[jax.ShapeDtypeStruct((n_seq, t_new, width), F32), jax.ShapeDtypeStruct(h.shape, F32)],
        compiler_params=_params("parallel"), name="sb_sample")(
            pt_flat, bias_cols, q, k_new, v_new, gate, h, *weights,
            *([cache_kt] * n_pages), *([cache_vt] * n_pages))


def _trunk(x, s_init, past, wts, n_seq, t_len, tm):
    (g_norm_a, w_main_a, w_lr_a, w_a2, b_a, g_head_a, w_out_a, g_kv, w_kv, w_kt, w_vt, g_norm_b, w_in_b, b_sb,
     w_out_b, g_final) = wts
    n_a = w_main_a.shape[0]
    n_b = w_in_b.shape[0]
    is_prompt = past is None
    h, a, w_o = x, None, None
    states = None
    for i in range(n_a):
        h, qk, v, gate, la = _gla_in_stage(h, a, w_o, g_norm_a[i], w_main_a[i], w_lr_a[i], w_a2[i], b_a[i], tm)
        if is_prompt:
            a, states = _gla(qk, v, la, gate, g_head_a[i], None, i, states, n_seq, t_len, GLA_CHUNK,
                             8 * GLA_CHUNK, 1)
        else:
            def padt(u):
                u = u.reshape(n_seq, t_len, u.shape[-1])
                u = jnp.pad(u, ((0, 0), (0, SAMPLE_CHUNK - t_len), (0, 0)))
                return u.reshape(n_seq * SAMPLE_CHUNK, u.shape[-1])
            a, states = _gla(padt(qk), padt(v), padt(la), padt(gate), g_head_a[i], s_init, i, states, n_seq,
                             SAMPLE_CHUNK, SAMPLE_CHUNK, SAMPLE_CHUNK, 4)
            a = a.reshape(n_seq, SAMPLE_CHUNK, -1)[:, :t_len].reshape(n_seq * t_len, -1)
        w_o = w_out_a[i]
    if is_prompt:
        for i in range(n_b):
            if i == 0:
                h, k_sh, v_sh, ktb, vtb, q, gate = _sb_in_stage(h, a, w_o, (g_kv, w_kt, w_vt), g_norm_b[i],
                                                                w_in_b[i], SB_KEY_BLOCK, t_len)
            else:
                h, q, gate = _sb_in_stage(h, a, w_o, None, g_norm_b[i], w_in_b[i], tm)
            a = _sb_prompt(q, ktb, vtb, gate, b_sb[i], n_seq, t_len, 2 * SB_KEY_BLOCK, SB_KEY_BLOCK)
            w_o = w_out_b[i]
    else:
        cache_k, cache_v, page_table = past
        h, k_sh, v_sh, _, _, q, gate = _sb_in_stage(h, a, w_o, (g_kv, w_kv), g_norm_b[0], w_in_b[0], tm)
        width = q.shape[1]
        seq3 = lambda u: u.reshape(n_seq, t_len, u.shape[-1])
        pad8 = lambda u: jnp.pad(seq3(u), ((0, 0), (0, 8 - t_len), (0, 0)))
        bias_cols = jnp.stack([jnp.tile(b_sb[i], t_len) for i in range(n_b)], axis=1)
        later = [(w_out_b[i - 1], g_norm_b[i], w_in_b[i]) for i in range(1, n_b)]
        a, h = _sb_sample(seq3(q.astype(F32)), pad8(k_sh), pad8(v_sh), seq3(gate), seq3(h), bias_cols, later,
                          cache_k, cache_v, page_table)
        a, h = a.reshape(n_seq * t_len, width), h.reshape(n_seq * t_len, -1)
        w_o = w_out_b[n_b - 1]
    y = _final_stage(h, a, w_o, g_final, tm)
    return y, states, k_sh, v_sh


def kernel(x_prompt, x_sample, state_gla, cache_k, cache_v, page_table, g_norm_a, w_in_a, w_alpha2, b_alpha,
           g_head_a, w_out_a, g_kv, w_kv, g_norm_b, w_in_b, b_sb, w_out_b, g_final):
    n_p, t_p, d = x_prompt.shape
    n_s, t_s, _ = x_sample.shape
    n_main = w_in_a.shape[2] - w_alpha2.shape[1]
    width = w_kv.shape[1] // 2
    sb_shape = (SB_HEADS, width // SB_HEADS)
    row = lambda g: g[..., None, :]
    wts = (row(g_norm_a), w_in_a[:, :, :n_main].astype(BF16), w_in_a[:, :, n_main:].astype(BF16),
           w_alpha2.astype(BF16), row(b_alpha), row(g_head_a), w_out_a.astype(BF16), row(g_kv),
           w_kv.astype(BF16), w_kv[:, :width].T.astype(BF16), w_kv[:, width:].T.astype(BF16),
           row(g_norm_b), w_in_b.astype(BF16), b_sb, w_out_b.astype(BF16), row(g_final))
    y_p, s_p, k_p, v_p = _trunk(x_prompt.reshape(n_p * t_p, d), None, None, wts, n_p, t_p, 512)
    n_phys, page = cache_k.shape[:2]
    to_t = lambda c: jnp.transpose(c, (0, 2, 3, 1)).reshape(n_phys, -1, page)
    past = (to_t(cache_k), to_t(cache_v), page_table)
    y_s, s_s, k_s, v_s = _trunk(x_sample.reshape(n_s * t_s, d), state_gla, past, wts, n_s, t_s, 256)
    from_t = lambda u: jnp.transpose(u.reshape(n_p, *sb_shape, t_p), (0, 3, 1, 2))
    return (y_p.reshape(n_p, t_p, d), y_s.reshape(n_s, t_s, d), s_p, s_s, from_t(k_p), from_t(v_p),
            k_s.reshape(n_s, t_s, *sb_shape), v_s.reshape(n_s, t_s, *sb_shape))
```
